```python
import jax, jax.numpy as jnp
from jax import lax
import numpy as np

D_MODEL = 1024
BATCH = 8
SEQ = 2048
DEPTH = 4

RET_HEADS = 4
RET_DK = 128
RET_DV = 256
RET_QK = RET_HEADS * RET_DK
RET_V = RET_HEADS * RET_DV
ROPE_BASE = 10000.0
MAX_POS_OFFSET = 1024
LRU_WIDTH = D_MODEL
LRU_BLOCKS = 16
LRU_BLOCK = LRU_WIDTH // LRU_BLOCKS
LRU_C = 8.0
CONV_W = 4
MLSTM_HEADS = 4
MLSTM_WIDTH = D_MODEL
MLSTM_DH = MLSTM_WIDTH // MLSTM_HEADS
QKV_BLOCK = 4
CHUNK = 64
D_FF = 4 * D_MODEL
N_BRANCH = 3
EPS = 1e-6

IN_SPLITS = (RET_QK, RET_QK, RET_V, RET_V,
             LRU_WIDTH, LRU_WIDTH,
             MLSTM_WIDTH, MLSTM_WIDTH,
             N_BRANCH * D_MODEL)
D_IN = int(sum(IN_SPLITS))
IN_OFFSETS = tuple(int(v) for v in np.cumsum(IN_SPLITS)[:-1])

kernel_name = "hybrid_retention_rglru_mlstm_trunk"


def rms_norm(x, gain):
    xf = x.astype(jnp.float32)
    xf = xf * lax.rsqrt(jnp.mean(xf * xf, axis=-1, keepdims=True) + EPS)
    return xf.astype(x.dtype) * gain


def head_rms_norm(xh):
    xf = xh.astype(jnp.float32)
    xf = xf * lax.rsqrt(jnp.mean(xf * xf, axis=-1, keepdims=True) + EPS)
    return xf.astype(xh.dtype)


def causal_depthwise_conv(x, w, b):
    S = x.shape[1]
    xp = jnp.pad(x, ((0, 0), (CONV_W - 1, 0), (0, 0)))
    y = b
    for k in range(CONV_W):
        y = y + w[k] * xp[:, k:k + S]
    return y


def block_diag_linear(x, w):
    B, S, _ = x.shape
    nb, bs, bo = w.shape
    return jnp.einsum('bsnd,nde->bsne', x.reshape(B, S, nb, bs), w).reshape(B, S, nb * bo)


def rotary(x, positions):
    half = x.shape[-1] // 2
    inv_freq = ROPE_BASE ** (-jnp.arange(half, dtype=jnp.float32) / half)
    ang = positions.astype(jnp.float32)[..., None] * inv_freq
    cos = jnp.cos(ang)[:, :, None, :].astype(x.dtype)
    sin = jnp.sin(ang)[:, :, None, :].astype(x.dtype)
    x1, x2 = x[..., :half], x[..., half:]
    return jnp.concatenate([x1 * cos - x2 * sin, x1 * sin + x2 * cos], axis=-1)


def to_chunks(x):
    B, S, H, d = x.shape
    return x.reshape(B, S // CHUNK, CHUNK, H, d).transpose(1, 0, 3, 2, 4)


def gate_to_chunks(g):
    B, S, H = g.shape
    return g.reshape(B, S // CHUNK, CHUNK, H).transpose(1,0, 3, 2)


def from_chunks(y):
    N, B, H, C, d = y.shape
    return y.transpose(1, 0, 3, 2, 4).reshape(B, N * C, H, d)


def chunked_retention(q, k, v):
    dtype = v.dtype
    q, k, v = q.astype(jnp.float32), k.astype(jnp.float32) * RET_DK ** -0.5, v.astype(jnp.float32)
    B = q.shape[0]
    log_gamma = jnp.log1p(-jnp.exp2(-5.0 - jnp.arange(RET_HEADS, dtype=jnp.float32)))
    pos = jnp.arange(CHUNK, dtype=jnp.float32)
    diff = pos[:, None] - pos[None, :]
    causal = diff >= 0
    decay_intra = jnp.where(causal, jnp.exp(log_gamma[:, None, None] * jnp.where(causal, diff, 0.0)), 0.0)
    decay_q = jnp.exp(log_gamma[:, None] * (pos + 1.0))[:, :, None]
    decay_k = jnp.exp(log_gamma[:, None] * (CHUNK - 1.0 - pos))[:, :, None]
    decay_chunk = jnp.exp(log_gamma * CHUNK)[:, None, None]

    def step(state, qkv):
        qc, kc, vc = qkv
        scores = jnp.einsum('bhid,bhjd->bhij', qc, kc) * decay_intra
        out = (jnp.einsum('bhij,bhje->bhie', scores, vc)
               + jnp.einsum('bhid,bhde->bhie', qc * decay_q, state))
        state = decay_chunk * state + jnp.einsum('bhjd,bhje->bhde', kc * decay_k, vc)
        return state, out

    state0 = jnp.zeros((B, RET_HEADS, RET_DK, RET_DV), jnp.float32)
    _, out = lax.scan(step, state0, (to_chunks(q), to_chunks(k), to_chunks(v)))
    return from_chunks(out).astype(dtype)


def rg_lru(x, w_r, b_r, w_i, b_i, lam):
    dtype = x.dtype
    xf = x.astype(jnp.float32)
    r = jax.nn.sigmoid(block_diag_linear(xf, w_r.astype(jnp.float32)) + b_r)
    i = jax.nn.sigmoid(block_diag_linear(xf, w_i.astype(jnp.float32)) + b_i)
    log_a = -LRU_C * r * jax.nn.softplus(-lam.astype(jnp.float32))
    a = jnp.exp(log_a)
    u = jnp.sqrt(-jnp.expm1(2.0 * log_a)) * (i * xf)

    def combine(left, right):
        a1, b1 = left
        a2, b2 = right
        return a1 * a2, a2 * b1 + b2

    _, h = lax.associative_scan(combine, (a, u), axis=1)
    return h.astype(dtype)


def chunked_mlstm(q, k, v, i_pre, f_pre):
    dtype = v.dtype
    q, k, v = q.astype(jnp.float32), k.astype(jnp.float32) * MLSTM_DH ** -0.5, v.astype(jnp.float32)
    i_pre = i_pre.astype(jnp.float32)
    log_f = jax.nn.log_sigmoid(f_pre.astype(jnp.float32))
    B = q.shape[0]
    causal = jnp.tril(jnp.ones((CHUNK, CHUNK), dtype=bool))

    def step(carry, inp):
        C_s, n_s, m_s = carry
        qc, kc, vc, ic, lfc = inp
        b = jnp.cumsum(lfc, axis=-1)
        D = jnp.where(causal, b[..., :, None] - b[..., None, :] + ic[..., None, :], -jnp.inf)
        inter = b + m_s[..., None]
        m_t = jnp.maximum(inter, jnp.max(D, axis=-1))
        w_inter = jnp.exp(inter - m_t)
        s = jnp.einsum('bhid,bhjd->bhij', qc, kc) * jnp.exp(D - m_t[..., None])
        num = (jnp.einsum('bhij,bhje->bhie', s, vc)
               + w_inter[..., None] * jnp.einsum('bhid,bhde->bhie', qc, C_s))
        den = jnp.sum(s, axis=-1) + w_inter * jnp.einsum('bhid,bhd->bhi', qc, n_s)
        h = num / jnp.maximum(jnp.abs(den), jnp.exp(-m_t))[..., None]
        b_last = b[..., -1]
        m_new = m_t[..., -1]
        w_k = jnp.exp(b_last[..., None] - b + ic - m_new[..., None])
        decay = jnp.exp(b_last + m_s - m_new)
        C_new = decay[..., None, None] * C_s + jnp.einsum('bhjd,bhje->bhde', kc * w_k[..., None], vc)
        n_new = decay[..., None] * n_s + jnp.einsum('bhjd,bhj->bhd', kc, w_k)
        return (C_new, n_new, m_new), h

    carry0 = (jnp.zeros((B, MLSTM_HEADS, MLSTM_DH, MLSTM_DH), jnp.float32),
              jnp.zeros((B, MLSTM_HEADS, MLSTM_DH), jnp.float32),
              jnp.zeros((B, MLSTM_HEADS), jnp.float32))
    _, h = lax.scan(step, carry0, (to_chunks(q), to_chunks(k), to_chunks(v),
                                   gate_to_chunks(i_pre), gate_to_chunks(log_f)))
    return from_chunks(h).astype(dtype)


def hybrid_mixer(h, positions, w_in, lru_conv_w, lru_conv_b, lru_w_r, lru_b_r, lru_w_i, lru_b_i,
                 lru_lambda, m_conv_w, m_conv_b, m_w_q, m_w_k, m_w_v, m_w_if, m_b_if, m_norm,
                 w_br_ret, w_br_lru, w_br_mlstm, w_out):
    B, S, _ = h.shape
    proj = h @ w_in
    rq, rk, rv, rg, lx, ly, mx, mo, gate_pre = jnp.split(proj, IN_OFFSETS, axis=-1)

    q = rotary(rq.reshape(B, S, RET_HEADS, RET_DK), positions)
    k = rotary(rk.reshape(B, S, RET_HEADS, RET_DK), positions)
    v = rv.reshape(B, S, RET_HEADS, RET_DV)
    ret = head_rms_norm(chunked_retention(q, k, v)).reshape(B, S, RET_V) * jax.nn.silu(rg)

    xl = causal_depthwise_conv(lx, lru_conv_w, lru_conv_b)
    lru = rg_lru(xl, lru_w_r, lru_b_r, lru_w_i, lru_b_i, lru_lambda) * jax.nn.gelu(ly)

    xc = jax.nn.silu(causal_depthwise_conv(mx, m_conv_w, m_conv_b))
    mq = block_diag_linear(xc, m_w_q)
    mk = block_diag_linear(xc, m_w_k)
    mv = block_diag_linear(mx, m_w_v)
    if_pre = jnp.concatenate([mq, mk, mv], axis=-1) @ m_w_if + m_b_if
    i_pre, f_pre = jnp.split(if_pre, 2, axis=-1)
    hm = chunked_mlstm(mq.reshape(B, S, MLSTM_HEADS, MLSTM_DH), mk.reshape(B, S, MLSTM_HEADS, MLSTM_DH),
                       mv.reshape(B, S, MLSTM_HEADS, MLSTM_DH), i_pre, f_pre)
    hm = jax.nn.sigmoid(mo) * hm.reshape(B, S, MLSTM_WIDTH)
    mls = head_rms_norm(hm.reshape(B, S, MLSTM_HEADS, MLSTM_DH)).reshape(B, S, MLSTM_WIDTH) * m_norm

    g_ret, g_lru, g_mls = jnp.split(jax.nn.sigmoid(gate_pre), N_BRANCH, axis=-1)
    merged = g_ret * (ret @ w_br_ret) + g_lru * (lru @ w_br_lru) + g_mls * (mls @ w_br_mlstm)
    return merged @ w_out


def squared_relu_mlp(h, w1, w2):
    return jnp.square(jax.nn.relu(h @ w1)) @ w2


def setup_inputs(seed: int = 0) -> dict:
    key = jax.random.key(seed)
    ks = iter(jax.random.split(key, 40))

    def nrm(shape, scale):
        return scale * jax.random.normal(next(ks), shape, jnp.float32)

    L, D, H = DEPTH, D_MODEL, MLSTM_HEADS
    x = nrm((BATCH, SEQ, D), 1.0)
    c = nrm((BATCH, D), 1.0)
    offset = jax.random.randint(next(ks), (BATCH, 1), 0, MAX_POS_OFFSET, dtype=jnp.int32)
    positions = offset + jnp.arange(SEQ, dtype=jnp.int32)[None, :]
    w_ada = nrm((L, D, 6 * D), 0.5 * D ** -0.5)
    b_ada = nrm((L, 6 * D), 0.02)
    norm_mix = 1.0 + nrm((L, D), 0.02)
    norm_mlp = 1.0 + nrm((L, D), 0.02)
    w_in = nrm((L, D, D_IN), D ** -0.5)
    lru_conv_w = nrm((L, CONV_W, LRU_WIDTH), CONV_W ** -0.5)
    lru_conv_b = nrm((L, LRU_WIDTH), 0.02)
    lru_w_r = nrm((L, LRU_BLOCKS, LRU_BLOCK, LRU_BLOCK), LRU_BLOCK ** -0.5)
    lru_b_r = nrm((L, LRU_WIDTH), 0.1)
    lru_w_i = nrm((L, LRU_BLOCKS, LRU_BLOCK, LRU_BLOCK), LRU_BLOCK ** -0.5)
    lru_b_i = nrm((L, LRU_WIDTH), 0.1)
    a_pow_c = jax.random.uniform(next(ks), (L, LRU_WIDTH), jnp.float32, 0.9, 0.999)
    a_base = a_pow_c ** (1.0 / LRU_C)
    lru_lambda = jnp.log(a_base) - jnp.log1p(-a_base)
    m_conv_w = nrm((L, CONV_W, MLSTM_WIDTH), CONV_W ** -0.5)
    m_conv_b = nrm((L, MLSTM_WIDTH), 0.02)
    nqb = MLSTM_WIDTH // QKV_BLOCK
    m_w_q = nrm((L, nqb, QKV_BLOCK, QKV_BLOCK), QKV_BLOCK ** -0.5)
    m_w_k = nrm((L, nqb, QKV_BLOCK, QKV_BLOCK), QKV_BLOCK ** -0.5)
    m_w_v = nrm((L, nqb, QKV_BLOCK, QKV_BLOCK), QKV_BLOCK ** -0.5)
    m_w_if = nrm((L, 3 * MLSTM_WIDTH, 2 * H), (3 * MLSTM_WIDTH) ** -0.5)
    m_b_if = jnp.concatenate([nrm((L, H), 0.1),
                              jnp.linspace(3.0, 6.0, H, dtype=jnp.float32)[None, :] + nrm((L, H), 0.1)], axis=-1)
    m_norm = 1.0 + nrm((L, MLSTM_WIDTH), 0.02)
    w_br_ret = nrm((L, RET_V, D), RET_V ** -0.5)
    w_br_lru = nrm((L, LRU_WIDTH, D), LRU_WIDTH ** -0.5)
    w_br_mlstm = nrm((L, MLSTM_WIDTH, D), MLSTM_WIDTH ** -0.5)
    w_out = nrm((L, D, D), D ** -0.5)
    w_ff1 = nrm((L, D, D_FF), D ** -0.5)
    w_ff2 = nrm((L, D_FF, D), D_FF ** -0.5)
    final_norm = 1.0 + nrm((D,), 0.02)
    return {"x": x, "c": c, "positions": positions, "w_ada": w_ada, "b_ada": b_ada,
            "norm_mix": norm_mix, "norm_mlp": norm_mlp, "w_in": w_in,
            "lru_conv_w": lru_conv_w, "lru_conv_b": lru_conv_b, "lru_w_r": lru_w_r, "lru_b_r": lru_b_r,
            "lru_w_i": lru_w_i, "lru_b_i": lru_b_i, "lru_lambda": lru_lambda,
            "m_conv_w": m_conv_w, "m_conv_b": m_conv_b, "m_w_q": m_w_q, "m_w_k": m_w_k, "m_w_v": m_w_v,
            "m_w_if": m_w_if, "m_b_if": m_b_if, "m_norm": m_norm,
            "w_br_ret": w_br_ret, "w_br_lru": w_br_lru, "w_br_mlstm": w_br_mlstm, "w_out": w_out,
            "w_ff1": w_ff1, "w_ff2": w_ff2, "final_norm": final_norm}


def reference(x, c, positions, w_ada, b_ada, norm_mix, norm_mlp, w_in,
              lru_conv_w, lru_conv_b, lru_w_r, lru_b_r, lru_w_i, lru_b_i, lru_lambda,
              m_conv_w, m_conv_b, m_w_q, m_w_k, m_w_v, m_w_if, m_b_if, m_norm,
              w_br_ret, w_br_lru, w_br_mlstm, w_out, w_ff1, w_ff2, final_norm):
    cond = jax.nn.silu(c)
    for l in range(DEPTH):
        mod = (cond @ w_ada[l] + b_ada[l])[:, None, :]
        sh1, sc1, g1, sh2, sc2, g2 = jnp.split(mod, 6, axis=-1)
        h = rms_norm(x, norm_mix[l]) * (1.0 + sc1) + sh1
        x = x + g1 * hybrid_mixer(h, positions, w_in[l], lru_conv_w[l], lru_conv_b[l], lru_w_r[l], lru_b_r[l],
                                  lru_w_i[l], lru_b_i[l], lru_lambda[l], m_conv_w[l], m_conv_b[l],
                                  m_w_q[l], m_w_k[l], m_w_v[l], m_w_if[l], m_b_if[l], m_norm[l],
                                  w_br_ret[l], w_br_lru[l], w_br_mlstm[l], w_out[l])
        h = rms_norm(x, norm_mlp[l]) * (1.0 + sc2) + sh2
        x = x + g2 * squared_relu_mlp(h, w_ff1[l], w_ff2[l])
    return rms_norm(x, final_norm)
```

```python
import functools

import jax
import jax.numpy as jnp
import numpy as np
from jax import lax
from jax.experimental import pallas as pl
from jax.experimental.pallas import tpu as pltpu

F32 = jnp.float32
BF16 = jnp.bfloat16

D_MODEL = 1024
RET_HEADS = 4
RET_DK = 128
RET_DV = 256
ROPE_BASE = 10000.0
LRU_BLOCK = 64
LRU_C = 8.0
CONV_W = 4
MLSTM_HEADS = 4
MLSTM_DH = 256
QKV_BLOCK = 4
CHUNK = 64
D_FF = 4 * D_MODEL
EPS = 1e-6

OFF_RQ, OFF_RK, OFF_RV, OFF_RG = 0, 512, 1024, 2048
OFF_LX, OFF_LY, OFF_MX, OFF_MO, OFF_GATE = 3072, 4096, 5120, 6144, 7168
D_IN = 10240

V7X_LANES = 128
V7X_SUBLANES = 8
V7X_VMEM_LIMIT_BYTES = 56 * 1024 * 1024


def _cparams(sem):
    return pltpu.CompilerParams(dimension_semantics=sem, vmem_limit_bytes=V7X_VMEM_LIMIT_BYTES)


def _dot(a, b):
    return jnp.dot(a, b, preferred_element_type=F32)


def _dot_nt(a, b):
    return lax.dot_general(a, b, (((1,), (1,)), ((), ())), preferred_element_type=F32)


def _dot_tn(a, b):
    return lax.dot_general(a, b, (((0,), (0,)), ((), ())), preferred_element_type=F32)


def _sigmoid(x):
    return 1.0 / (1.0 + jnp.exp(-x))


def _silu(x):
    return x * _sigmoid(x)


def _ada_kernel(c_ref, w_ref, b_ref, o_ref):
    cond = _silu(c_ref[...])
    o_ref[0] = _dot(cond.astype(BF16), w_ref[0].astype(BF16)) + b_ref[0]


def _ada(c, w_ada, b_ada):
    depth, d, n = w_ada.shape
    bsz = c.shape[0]
    tn = 1024
    return pl.pallas_call(
        _ada_kernel,
        out_shape=jax.ShapeDtypeStruct((depth, bsz, n), F32),
        grid=(depth, n // tn),
        in_specs=[
            pl.BlockSpec((bsz, d), lambda l, j: (0, 0)),
            pl.BlockSpec((1, d, tn), lambda l, j: (l, 0, j)),
            pl.BlockSpec((1, 1, tn), lambda l, j: (l, 0, j)),
        ],
        out_specs=pl.BlockSpec((1, bsz, tn), lambda l, j: (l, 0, j)),
        compiler_params=_cparams(("parallel", "parallel")),
        name="ada_mod",
    )(c, w_ada, b_ada.reshape(depth, 1, n))


def _rope_kernel(pos_ref, invf_ref, cc_ref, ss_ref):
    ang = pos_ref[0] * invf_ref[...]
    lane = lax.broadcasted_iota(jnp.int32, ang.shape, 1)
    sn = jnp.sin(ang)
    cc_ref[0] = jnp.cos(ang)
    ss_ref[0] = jnp.where(lane < RET_DK // 2, -sn, sn)


def _rope_tables(positions):
    bsz, seq = positions.shape
    half = RET_DK // 2
    inv_freq = ROPE_BASE ** (-jnp.arange(half, dtype=F32) / half)
    invf = jnp.concatenate([inv_freq, inv_freq]).reshape(1, RET_DK)
    pos = positions.astype(F32).reshape(bsz, seq, 1)
    ts = min(seq, 512)
    out = jax.ShapeDtypeStruct((bsz, seq, RET_DK), F32)
    return pl.pallas_call(
        _rope_kernel,
        out_shape=(out, out),
        grid=(bsz, seq // ts),
        in_specs=[
            pl.BlockSpec((1, ts, 1), lambda b, t: (b, t, 0)),
            pl.BlockSpec((1, RET_DK), lambda b, t: (0, 0)),
        ],
        out_specs=(pl.BlockSpec((1, ts, RET_DK), lambda b, t: (b, t, 0)),
                   pl.BlockSpec((1, ts, RET_DK), lambda b, t: (b, t, 0))),
        compiler_params=_cparams(("parallel", "parallel")),
        name="rope_tables",
    )(pos, invf)


def _modulated_norm(x, gain, scale, shift):
    xn = x * lax.rsqrt(jnp.mean(x * x, axis=-1, keepdims=True) + EPS)
    return xn * gain * (1.0 + scale) + shift


def _inproj_kernel(x_ref, mod_ref, gain_ref, w_ref, o_ref, h_ref):
    @pl.when(pl.program_id(2) == 0)
    def _():
        h = _modulated_norm(x_ref[0], gain_ref[0], mod_ref[0, 0, 1:2, :], mod_ref[0, 0, 0:1, :])
        h_ref[...] = h.astype(BF16)

    o_ref[0] = _dot(h_ref[...], w_ref[0]).astype(BF16)


def _inproj(x, mod, gain, w_in, l):
    bsz, seq, d = x.shape
    n = w_in.shape[-1]
    tm = min(seq, 1024)
    tn = 2048
    return pl.pallas_call(
        _inproj_kernel,
        out_shape=jax.ShapeDtypeStruct((bsz, seq, n), BF16),
        grid=(bsz, seq // tm, n // tn),
        in_specs=[
            pl.BlockSpec((1, tm, d), lambda b, i, j: (b, i, 0)),
            pl.BlockSpec((1, 1, 6, d), lambda b, i, j: (l, b, 0, 0)),
            pl.BlockSpec((1, 1, d), lambda b, i, j: (l, 0, 0)),
            pl.BlockSpec((1, d, tn), lambda b, i, j: (l, 0, j)),
        ],
        out_specs=pl.BlockSpec((1, tm, tn), lambda b, i, j: (b, i, j)),
        scratch_shapes=[pltpu.VMEM((tm, d), BF16)],
        compiler_params=_cparams(("parallel", "parallel", "arbitrary")),
        name="in_proj",
    )(x, mod, gain, w_in)


def _ret_kernel(q_ref, k_ref, v_ref, g_ref, cc_ref, ss_ref, o_ref, state_ref):
    head = pl.program_id(1)

    @pl.when(pl.program_id(2) == 0)
    def _():
        state_ref[...] = jnp.zeros_like(state_ref)

    hv = jnp.zeros((1, 1), F32) + head.astype(F32)
    log_gamma = jnp.log1p(-jnp.exp2(-5.0 - hv))
    ii = lax.broadcasted_iota(jnp.int32, (CHUNK, CHUNK), 0)
    jj = lax.broadcasted_iota(jnp.int32, (CHUNK, CHUNK), 1)
    causal = ii >= jj
    diff = jnp.where(causal, (ii - jj).astype(F32), 0.0)
    decay_intra = jnp.where(causal, jnp.exp(log_gamma * diff), 0.0)
    pos = lax.broadcasted_iota(jnp.int32, (CHUNK, 1), 0).astype(F32)
    decay_q = jnp.exp(log_gamma * (pos + 1.0))
    decay_k = jnp.exp(log_gamma * (CHUNK - 1.0 - pos))
    decay_chunk = jnp.exp(log_gamma * CHUNK)

    cc = cc_ref[0]
    ss = ss_ref[0]
    q = q_ref[0].astype(F32)
    k = k_ref[0].astype(F32)
    q = q * cc + pltpu.roll(q, RET_DK // 2, 1) * ss
    k = (k * cc + pltpu.roll(k, RET_DK // 2, 1) * ss) * RET_DK ** -0.5

    ts = q.shape[0]
    for c in range(ts // CHUNK):
        sl = slice(c * CHUNK, (c + 1) * CHUNK)
        qc, kc = q[sl], k[sl]
        vc = v_ref[0, sl, :]
        scores = _dot_nt(qc.astype(BF16), kc.astype(BF16)) * decay_intra
        state = state_ref[...]
        out = _dot(scores.astype(BF16), vc) + _dot((qc * decay_q).astype(BF16), state.astype(BF16))
        state_ref[...] = decay_chunk * state + _dot_tn((kc * decay_k).astype(BF16), vc)
        out = out * lax.rsqrt(jnp.mean(out * out, axis=-1, keepdims=True) + EPS)
        o_ref[0, sl, :] = (out * _silu(g_ref[0, sl, :].astype(F32))).astype(BF16)


def _retention(proj, cc, ss):
    bsz, seq, _ = proj.shape
    ts = min(seq, 512)
    qb, kb = OFF_RQ // RET_DK, OFF_RK // RET_DK
    vb, gb = OFF_RV // RET_DV, OFF_RG // RET_DV
    return pl.pallas_call(
        _ret_kernel,
        out_shape=jax.ShapeDtypeStruct((bsz, seq, RET_HEADS * RET_DV), BF16),
        grid=(bsz, RET_HEADS, seq // ts),
        in_specs=[
            pl.BlockSpec((1, ts, RET_DK), lambda b, h, t: (b, t, qb + h)),
            pl.BlockSpec((1, ts, RET_DK), lambda b, h, t: (b, t, kb + h)),
            pl.BlockSpec((1, ts, RET_DV), lambda b, h, t: (b, t, vb + h)),
            pl.BlockSpec((1, ts, RET_DV), lambda b, h, t: (b, t, gb + h)),
            pl.BlockSpec((1, ts, RET_DK), lambda b, h, t: (b, t, 0)),
            pl.BlockSpec((1, ts, RET_DK), lambda b, h, t: (b, t, 0)),
        ],
        out_specs=pl.BlockSpec((1, ts, RET_DV), lambda b, h, t: (b, t, h)),
        scratch_shapes=[pltpu.VMEM((RET_DK, RET_DV), F32)],
        compiler_params=_cparams(("parallel", "parallel", "arbitrary")),
        name="retention",
    )(proj, proj, proj, proj, cc, ss)


def _causal_conv(x, prev, w, b):
    rows = x.shape[0]
    row8 = lax.broadcasted_iota(jnp.int32, (V7X_SUBLANES, x.shape[1]), 0)
    y = b
    for k in range(CONV_W):
        s = CONV_W - 1 - k
        if s == 0:
            xs = x
        else:
            rolled = pltpu.roll(x, s, 0)
            top = jnp.where(row8 < s, pltpu.roll(prev, s, 0), rolled[:V7X_SUBLANES])
            xs = jnp.concatenate([top, rolled[V7X_SUBLANES:]], axis=0) if rows > V7X_SUBLANES else top
        y = y + w[k:k + 1, :] * xs
    return y


def _lru_kernel(lx_ref, ly_ref, cw_ref, cb_ref, wr_ref, br_ref, wi_ref, bi_ref, lam_ref, o_ref,
                al_ref, ul_ref, carry_ref):
    x = lx_ref[0].astype(F32)
    seq, width = x.shape
    groups = seq // V7X_SUBLANES
    xl = _causal_conv(x, jnp.zeros((V7X_SUBLANES, width), F32), cw_ref[0], cb_ref[0])
    xb = xl.astype(BF16)
    r = _sigmoid(_dot(xb, wr_ref[0, 0]) + br_ref[0])
    i = _sigmoid(_dot(xb, wi_ref[0, 0]) + bi_ref[0])
    lam = lam_ref[0]
    softplus_neg_lam = jnp.maximum(-lam, 0.0) + jnp.log1p(jnp.exp(-jnp.abs(lam)))
    log_a = -LRU_C * r * softplus_neg_lam
    a = jnp.exp(log_a)
    u = jnp.sqrt(-jnp.tanh(log_a) * (a * a + 1.0)) * (i * xl)

    row = lax.broadcasted_iota(jnp.int32, (seq, width), 0)
    r8 = jnp.bitwise_and(row, V7X_SUBLANES - 1)
    s = 1
    while s < V7X_SUBLANES:
        m = r8 >= s
        a_sh = pltpu.roll(a, s, 0)
        u_sh = pltpu.roll(u, s, 0)
        u = jnp.where(m, a * u_sh + u, u)
        a = jnp.where(m, a * a_sh, a)
        s *= 2

    a3 = a.reshape(groups, V7X_SUBLANES, width)
    u3 = u.reshape(groups, V7X_SUBLANES, width)
    al_ref[...] = jnp.broadcast_to(a3[:, V7X_SUBLANES - 1:, :], a3.shape)
    ul_ref[...] = jnp.broadcast_to(u3[:, V7X_SUBLANES - 1:, :], u3.shape)

    def body(g, carry):
        carry_ref[g] = carry
        return al_ref[g] * carry + ul_ref[g]

    lax.fori_loop(0, groups, body, jnp.zeros((V7X_SUBLANES, width), F32), unroll=8)
    h = (a3 * carry_ref[...] + u3).reshape(seq, width)
    o_ref[0] = (h * jax.nn.gelu(ly_ref[0].astype(F32))).astype(BF16)


def _lru(proj, conv_w, conv_b, w_r, b_r, w_i, b_i, lam, l):
    bsz, seq, _ = proj.shape
    w = V7X_LANES
    nblk = D_MODEL // w
    xb, yb = OFF_LX // w, OFF_LY // w
    groups = seq // V7X_SUBLANES
    vec = lambda: pl.BlockSpec((1, 1, w), lambda b, j: (l, 0, j))
    return pl.pallas_call(
        _lru_kernel,
        out_shape=jax.ShapeDtypeStruct((bsz, seq, D_MODEL), BF16),
        grid=(bsz, nblk),
        in_specs=[
            pl.BlockSpec((1, seq, w), lambda b, j: (b, 0, xb + j)),
            pl.BlockSpec((1, seq, w), lambda b, j: (b, 0, yb + j)),
            pl.BlockSpec((1, CONV_W, w), lambda b, j: (l, 0, j)),
            vec(),
            pl.BlockSpec((1, 1, w, w), lambda b, j: (l, j, 0, 0)),
            vec(),
            pl.BlockSpec((1, 1, w, w), lambda b, j: (l, j, 0, 0)),
            vec(),
            vec(),
        ],
        out_specs=pl.BlockSpec((1, seq, w), lambda b, j: (b, 0, j)),
        scratch_shapes=[pltpu.VMEM((groups, V7X_SUBLANES, w), F32)] * 3,
        compiler_params=_cparams(("parallel", "parallel")),
        name="rg_lru",
    )(proj, proj, conv_w, conv_b, w_r, b_r, w_i, b_i, lam)


def _mprep_kernel(mx_ref, cw_ref, cb_ref, wq_ref, wk_ref, wv_ref, wif_ref, bif_ref,
                  q_ref, k_ref, v_ref, gate_ref, tail_ref):
    @pl.when(pl.program_id(1) == 0)
    def _():
        tail_ref[...] = jnp.zeros_like(tail_ref)

    mxb = mx_ref[0]
    mx = mxb.astype(F32)
    ts = mx.shape[0]
    y = _causal_conv(mx, tail_ref[...], cw_ref[0], cb_ref[0])
    tail_ref[...] = mx[ts - V7X_SUBLANES:]
    xc = _silu(y).astype(BF16)
    acc = jnp.zeros((ts, V7X_LANES), F32) + bif_ref[0]
    nh = MLSTM_HEADS
    for h in range(nh):
        cs = slice(h * MLSTM_DH, (h + 1) * MLSTM_DH)
        mq = _dot(xc[:, cs], wq_ref[0, h]).astype(BF16)
        mk = _dot(xc[:, cs], wk_ref[0, h]).astype(BF16)
        mv = _dot(mxb[:, cs], wv_ref[0, h]).astype(BF16)
        q_ref[0, :, cs] = mq
        k_ref[0, :, cs] = mk
        v_ref[0, :, cs] = mv
        acc = acc + _dot(mq, wif_ref[0, h]) + _dot(mk, wif_ref[0, nh + h]) + _dot(mv, wif_ref[0, 2 * nh + h])
    lane = lax.broadcasted_iota(jnp.int32, acc.shape, 1)
    log_f = jnp.minimum(acc, 0.0) - jnp.log1p(jnp.exp(-jnp.abs(acc)))
    gate_ref[0] = jnp.where(lane >= nh, log_f, acc)


def _mlstm_prep(proj, conv_w, conv_b, wq, wk, wv, wif, bif, l):
    bsz, seq, _ = proj.shape
    width = MLSTM_HEADS * MLSTM_DH
    ts = min(seq, 512)
    nh, dh = MLSTM_HEADS, MLSTM_DH
    qkv = jax.ShapeDtypeStruct((bsz, seq, width), BF16)
    wspec = lambda: pl.BlockSpec((1, nh, dh, dh), lambda b, t: (l, 0, 0, 0))
    ospec = lambda: pl.BlockSpec((1, ts, width), lambda b, t: (b, t, 0))
    return pl.pallas_call(
        _mprep_kernel,
        out_shape=(qkv, qkv, qkv, jax.ShapeDtypeStruct((bsz, seq, V7X_LANES), F32)),
        grid=(bsz, seq // ts),
        in_specs=[
            pl.BlockSpec((1, ts, width), lambda b, t: (b, t, OFF_MX // width)),
            pl.BlockSpec((1, CONV_W, width), lambda b, t: (l, 0, 0)),
            pl.BlockSpec((1, 1, width), lambda b, t: (l, 0, 0)),
            wspec(), wspec(), wspec(),
            pl.BlockSpec((1, 3 * nh, dh, V7X_LANES), lambda b, t: (l, 0, 0, 0)),
            pl.BlockSpec((1, 1, V7X_LANES), lambda b, t: (l, 0, 0)),
        ],
        out_specs=(ospec(), ospec(), ospec(),
                   pl.BlockSpec((1, ts, V7X_LANES), lambda b, t: (b, t, 0))),
        scratch_shapes=[pltpu.VMEM((V7X_SUBLANES, width), F32)],
        compiler_params=_cparams(("parallel", "arbitrary")),
        name="mlstm_prep",
    )(proj, conv_w, conv_b, wq, wk, wv, wif, bif)


def _mlstm_kernel(q_ref, k_ref, v_ref, gate_ref, mo_ref, mn_ref, o_ref, c_ref, n_ref, m_ref):
    head = pl.program_id(1)

    @pl.when(pl.program_id(2) == 0)
    def _():
        c_ref[...] = jnp.zeros_like(c_ref)
        n_ref[...] = jnp.zeros_like(n_ref)
        m_ref[...] = jnp.zeros_like(m_ref)

    gates = gate_ref[0]
    ts = gates.shape[0]
    lane = lax.broadcasted_iota(jnp.int32, gates.shape, 1)
    i_all = jnp.sum(jnp.where(lane == head, gates, 0.0), axis=1, keepdims=True)
    lf_all = jnp.sum(jnp.where(lane == head + MLSTM_HEADS, gates, 0.0), axis=1, keepdims=True)

    ii = lax.broadcasted_iota(jnp.int32, (CHUNK, CHUNK), 0)
    jj = lax.broadcasted_iota(jnp.int32, (CHUNK, CHUNK), 1)
    causal = ii >= jj
    diag = ii == jj

    for c in range(ts // CHUNK):
        sl = slice(c * CHUNK, (c + 1) * CHUNK)
        ic, lf = i_all[sl], lf_all[sl]
        b_row = jnp.sum(jnp.where(ii <= jj, lf, 0.0), axis=0, keepdims=True)
        lf_row = jnp.sum(jnp.where(diag, lf, 0.0), axis=0, keepdims=True)
        ic_row = jnp.sum(jnp.where(diag, ic, 0.0), axis=0, keepdims=True)
        b_col = jnp.sum(jnp.where(causal, lf_row, 0.0), axis=1, keepdims=True)
        dmat = jnp.where(causal, b_col - b_row + ic_row, -jnp.inf)
        m_s = m_ref[...]
        inter = b_col + m_s
        m_t = jnp.maximum(inter, jnp.max(dmat, axis=1, keepdims=True))
        w_inter = jnp.exp(inter - m_t)
        qc = q_ref[0, sl, :]
        kc = k_ref[0, sl, :].astype(F32) * MLSTM_DH ** -0.5
        vc = v_ref[0, sl, :]
        s = _dot_nt(qc, kc.astype(BF16)) * jnp.exp(dmat - m_t)
        c_s = c_ref[...]
        n_s = n_ref[...]
        num = _dot(s.astype(BF16), vc) + w_inter * _dot(qc, c_s.astype(BF16))
        den = (jnp.sum(s, axis=1, keepdims=True)
               + w_inter * jnp.sum(qc.astype(F32) * n_s, axis=1, keepdims=True))
        h = num / jnp.maximum(jnp.abs(den), jnp.exp(-m_t))
        b_last = b_col[CHUNK - 1:]
        m_new = m_t[CHUNK - 1:]
        w_k = jnp.exp(b_last - b_col + ic - m_new)
        decay = jnp.exp(b_last + m_s - m_new)
        kw = kc * w_k
        c_ref[...] = decay * c_s + _dot_tn(kw.astype(BF16), vc)
        n_ref[...] = decay * n_s + jnp.sum(kw, axis=0, keepdims=True)
        m_ref[...] = m_new
        o = _sigmoid(mo_ref[0, sl, :].astype(F32)) * h
        o = o * lax.rsqrt(jnp.mean(o * o, axis=-1, keepdims=True) + EPS)
        o_ref[0, sl, :] = (o * mn_ref[0]).astype(BF16)


def _mlstm(mq, mk, mv, gates, proj, m_norm, l):
    bsz, seq, width = mq.shape
    dh = MLSTM_DH
    ts = min(seq, 256)
    hspec = lambda: pl.BlockSpec((1, ts, dh), lambda b, h, t: (b, t, h))
    return pl.pallas_call(
        _mlstm_kernel,
        out_shape=jax.ShapeDtypeStruct((bsz, seq, width), BF16),
        grid=(bsz, MLSTM_HEADS, seq // ts),
        in_specs=[
            hspec(), hspec(), hspec(),
            pl.BlockSpec((1, ts, V7X_LANES), lambda b, h, t: (b, t, 0)),
            pl.BlockSpec((1, ts, dh), lambda b, h, t: (b, t, OFF_MO // dh + h)),
            pl.BlockSpec((1, 1, dh), lambda b, h, t: (l, 0, h)),
        ],
        out_specs=hspec(),
        scratch_shapes=[pltpu.VMEM((dh, dh), F32), pltpu.VMEM((1, dh), F32), pltpu.VMEM((1, 1), F32)],
        compiler_params=_cparams(("parallel", "parallel", "arbitrary")),
        name="mlstm",
    )(mq, mk, mv, gates, proj, m_norm)


def _merge_kernel(x_ref, ret_ref, lru_ref, mls_ref, g0_ref, g1_ref, g2_ref, mod_ref,
                  wr_ref, wl_ref, wm_ref, wo_ref, o_ref):
    merged = (_sigmoid(g0_ref[0].astype(F32)) * _dot(ret_ref[0], wr_ref[0])
              + _sigmoid(g1_ref[0].astype(F32)) * _dot(lru_ref[0], wl_ref[0])
              + _sigmoid(g2_ref[0].astype(F32)) * _dot(mls_ref[0], wm_ref[0]))
    y = _dot(merged.astype(BF16), wo_ref[0])
    o_ref[0] = x_ref[0] + mod_ref[0, 0, 2:3, :] * y


def _merge(x, ret, lru, mls, proj, mod, w_br_ret, w_br_lru, w_br_mlstm, w_out, l):
    bsz, seq, d = x.shape
    tm = min(seq, 512)
    gb = OFF_GATE // d
    tok = lambda: pl.BlockSpec((1, tm, d), lambda b, i: (b, i, 0))
    wsp = lambda: pl.BlockSpec((1, d, d), lambda b, i: (l, 0, 0))
    return pl.pallas_call(
        _merge_kernel,
        out_shape=jax.ShapeDtypeStruct((bsz, seq, d), F32),
        grid=(bsz, seq // tm),
        in_specs=[
            tok(), tok(), tok(), tok(),
            pl.BlockSpec((1, tm, d), lambda b, i: (b, i, gb)),
            pl.BlockSpec((1, tm, d), lambda b, i: (b, i, gb + 1)),
            pl.BlockSpec((1, tm, d), lambda b, i: (b, i, gb + 2)),
            pl.BlockSpec((1, 1, 6, d), lambda b, i: (l, b, 0, 0)),
            wsp(), wsp(), wsp(), wsp(),
        ],
        out_specs=tok(),
        compiler_params=_cparams(("parallel", "parallel")),
        name="merge_out",
    )(x, ret, lru, mls, proj, proj, proj, mod, w_br_ret, w_br_lru, w_br_mlstm, w_out)


def _ffn_kernel(x_ref, mod_ref, gain_ref, w1_ref, w2_ref, o_ref):
    x = x_ref[0]
    h = _modulated_norm(x, gain_ref[0], mod_ref[0, 0, 4:5, :], mod_ref[0, 0, 3:4, :])
    a = jnp.square(jnp.maximum(_dot(h.astype(BF16), w1_ref[0]), 0.0))
    o_ref[0] = x + mod_ref[0, 0, 5:6, :] * _dot(a.astype(BF16), w2_ref[0])


def _ffn(x, mod, gain, w1, w2, l):
    bsz, seq, d = x.shape
    dff = w1.shape[-1]
    tm = min(seq, 512)
    tok = lambda: pl.BlockSpec((1, tm, d), lambda b, i: (b, i, 0))
    return pl.pallas_call(
        _ffn_kernel,
        out_shape=jax.ShapeDtypeStruct((bsz, seq, d), F32),
        grid=(bsz, seq // tm),
        in_specs=[
            tok(),
            pl.BlockSpec((1, 1, 6, d), lambda b, i: (l, b, 0, 0)),
            pl.BlockSpec((1, 1, d), lambda b, i: (l, 0, 0)),
            pl.BlockSpec((1, d, dff), lambda b, i: (l, 0, 0), pipeline_mode=pl.Buffered(1)),
            pl.BlockSpec((1, dff, d), lambda b, i: (l, 0, 0), pipeline_mode=pl.Buffered(1)),
        ],
        out_specs=tok(),
        compiler_params=_cparams(("parallel", "parallel")),
        name="ffn",
    )(x, mod, gain, w1, w2)


def _final_norm_kernel(x_ref, gain_ref, o_ref):
    x = x_ref[0]
    o_ref[0] = x * lax.rsqrt(jnp.mean(x * x, axis=-1, keepdims=True) + EPS) * gain_ref[...]


def _final_norm(x, gain):
    bsz, seq, d = x.shape
    tm = min(seq, 1024)
    tok = pl.BlockSpec((1, tm, d), lambda b, i: (b, i, 0))
    return pl.pallas_call(
        _final_norm_kernel,
        out_shape=jax.ShapeDtypeStruct((bsz, seq, d), F32),
        grid=(bsz, seq // tm),
        in_specs=[tok, pl.BlockSpec((1, d), lambda b, i: (0, 0))],
        out_specs=tok,
        compiler_params=_cparams(("parallel", "parallel")),
        name="final_norm",
    )(x, gain.reshape(1, d))


def _block_diag_tiles(w, tile):
    depth, nb, bs, _ = w.shape
    per = tile // bs
    w5 = w.reshape(depth, nb // per, per, bs, bs)
    eye = jnp.eye(per, dtype=w.dtype)
    dense = jnp.einsum('ltnde,nm->ltndme', w5, eye)
    return dense.reshape(depth, nb // per, tile, tile)


def kernel(x, c, positions, w_ada, b_ada, norm_mix, norm_mlp, w_in, lru_conv_w, lru_conv_b, lru_w_r, lru_b_r, lru_w_i, lru_b_i, lru_lambda, m_conv_w, m_conv_b, m_w_q, m_w_k, m_w_v, m_w_if, m_b_if, m_norm, w_br_ret, w_br_lru, w_br_mlstm, w_out, w_ff1, w_ff2, final_norm):
    depth = w_in.shape[0]
    bsz, seq, d = x.shape
    nh, dh = MLSTM_HEADS, MLSTM_DH

    mod = _ada(c, w_ada, b_ada).reshape(depth, bsz, 6, d)
    cc, ss = _rope_tables(positions)

    vec = lambda a: a.reshape(depth, 1, a.shape[-1])
    w_in_b = w_in.astype(BF16)
    w_r_t = _block_diag_tiles(lru_w_r, V7X_LANES).astype(BF16)
    w_i_t = _block_diag_tiles(lru_w_i, V7X_LANES).astype(BF16)
    wq_t = _block_diag_tiles(m_w_q, dh).astype(BF16)
    wk_t = _block_diag_tiles(m_w_k, dh).astype(BF16)
    wv_t = _block_diag_tiles(m_w_v, dh).astype(BF16)
    wif_t = jnp.pad(m_w_if, ((0, 0), (0, 0), (0, V7X_LANES - 2 * nh))).reshape(depth, 3 * nh, dh, V7X_LANES).astype(BF16)
    bif_t = jnp.pad(m_b_if, ((0, 0), (0, V7X_LANES - 2 * nh))).reshape(depth, 1, V7X_LANES)
    w_br_ret_b, w_br_lru_b, w_br_mls_b = w_br_ret.astype(BF16), w_br_lru.astype(BF16), w_br_mlstm.astype(BF16)
    w_out_b, w_ff1_b, w_ff2_b = w_out.astype(BF16), w_ff1.astype(BF16), w_ff2.astype(BF16)
    g_mix, g_mlp = vec(norm_mix), vec(norm_mlp)
    l_cb, l_br, l_bi, l_lam = vec(lru_conv_b), vec(lru_b_r), vec(lru_b_i), vec(lru_lambda)
    m_cb, m_nrm = vec(m_conv_b), vec(m_norm)

    for l in range(depth):
        proj = _inproj(x, mod, g_mix, w_in_b, l)
        ret = _retention(proj, cc, ss)
        lru = _lru(proj, lru_conv_w, l_cb, w_r_t, l_br, w_i_t, l_bi, l_lam, l)
        mq, mk, mv, gates = _mlstm_prep(proj, m_conv_w, m_cb, wq_t, wk_t, wv_t, wif_t, bif_t, l)
        mls = _mlstm(mq, mk, mv, gates, proj, m_nrm, l)
        x = _merge(x, ret, lru, mls, proj, mod, w_br_ret_b, w_br_lru_b, w_br_mls_b, w_out_b, l)
        x = _ffn(x, mod, g_mlp, w_ff1_b, w_ff2_b, l)
    return _final_norm(x, final_norm)
```

```python
import functools

import jax
import jax.numpy as jnp
import numpy as np
from jax import lax
from jax.experimental import pallas as pl
from jax.experimental.pallas import tpu as pltpu

F32 = jnp.float32
BF16 = jnp.bfloat16

D_MODEL = 1024
RET_HEADS = 4
RET_DK = 128
RET_DV = 256
ROPE_BASE = 10000.0
LRU_BLOCK = 64
LRU_C = 8.0
CONV_W = 4
MLSTM_HEADS = 4
MLSTM_DH = 256
QKV_BLOCK = 4
CHUNK = 64
D_FF = 4 * D_MODEL
EPS = 1e-6

OFF_RQ, OFF_RK, OFF_RV, OFF_RG = 0, 512, 1024, 2048
OFF_LX, OFF_LY, OFF_MX, OFF_MO, OFF_GATE = 3072, 4096, 5120, 6144, 7168
D_IN = 10240

V7X_LANES = 128
V7X_SUBLANES = 8
V7X_VMEM_LIMIT_BYTES = 56 * 1024 * 1024


def _cparams(sem):
    return pltpu.CompilerParams(dimension_semantics=sem, vmem_limit_bytes=V7X_VMEM_LIMIT_BYTES)


def _dot(a, b):
    return jnp.dot(a, b, preferred_element_type=F32)


def _dot_nt(a, b):
    return lax.dot_general(a, b, (((1,), (1,)), ((), ())), preferred_element_type=F32)


def _dot_tn(a, b):
    return lax.dot_general(a, b, (((0,), (0,)), ((), ())), preferred_element_type=F32)


def _sigmoid(x):
    return 1.0 / (1.0 + jnp.exp(-x))


def _silu(x):
    return x * _sigmoid(x)


def _ada_kernel(c_ref, w_ref, b_ref, o_ref):
    cond = _silu(c_ref[...])
    o_ref[0] = _dot(cond.astype(BF16), w_ref[0].astype(BF16)) + b_ref[0]


def _ada(c, w_ada, b_ada):
    depth, d, n = w_ada.shape
    bsz = c.shape[0]
    tn = 1024
    return pl.pallas_call(
        _ada_kernel,
        out_shape=jax.ShapeDtypeStruct((depth, bsz, n), F32),
        grid=(depth, n // tn),
        in_specs=[
            pl.BlockSpec((bsz, d), lambda l, j: (0, 0)),
            pl.BlockSpec((1, d, tn), lambda l, j: (l, 0, j)),
            pl.BlockSpec((1, 1, tn), lambda l, j: (l, 0, j)),
        ],
        out_specs=pl.BlockSpec((1, bsz, tn), lambda l, j: (l, 0, j)),
        compiler_params=_cparams(("parallel", "parallel")),
        name="ada_mod",
    )(c, w_ada, b_ada.reshape(depth, 1, n))


def _rope_kernel(pos_ref, invf_ref, cc_ref, ss_ref):
    ang = pos_ref[0] * invf_ref[...]
    lane = lax.broadcasted_iota(jnp.int32, ang.shape, 1)
    sn = jnp.sin(ang)
    cc_ref[0] = jnp.cos(ang)
    ss_ref[0] = jnp.where(lane < RET_DK // 2, -sn, sn)


def _rope_tables(positions):
    bsz, seq = positions.shape
    half = RET_DK // 2
    inv_freq = ROPE_BASE ** (-jnp.arange(half, dtype=F32) / half)
    invf = jnp.concatenate([inv_freq, inv_freq]).reshape(1, RET_DK)
    pos = positions.astype(F32).reshape(bsz, seq, 1)
    ts = min(seq, 512)
    out = jax.ShapeDtypeStruct((bsz, seq, RET_DK), F32)
    return pl.pallas_call(
        _rope_kernel,
        out_shape=(out, out),
        grid=(bsz, seq // ts),
        in_specs=[
            pl.BlockSpec((1, ts, 1), lambda b, t: (b, t, 0)),
            pl.BlockSpec((1, RET_DK), lambda b, t: (0, 0)),
        ],
        out_specs=(pl.BlockSpec((1, ts, RET_DK), lambda b, t: (b, t, 0)),
                   pl.BlockSpec((1, ts, RET_DK), lambda b, t: (b, t, 0))),
        compiler_params=_cparams(("parallel", "parallel")),
        name="rope_tables",
    )(pos, invf)


def _modulated_norm(x, gain, scale, shift):
    xn = x * lax.rsqrt(jnp.mean(x * x, axis=-1, keepdims=True) + EPS)
    return xn * gain * (1.0 + scale) + shift


def _inproj_kernel(x_ref, mod_ref, gain_ref, w_ref, o_ref, h_ref):
    @pl.when(pl.program_id(2) == 0)
    def _():
        h = _modulated_norm(x_ref[0], gain_ref[0], mod_ref[0, 0, 1:2, :], mod_ref[0, 0, 0:1, :])
        h_ref[...] = h.astype(BF16)

    o_ref[0] = _dot(h_ref[...], w_ref[0]).astype(BF16)


def _inproj(x, mod, gain, w_in, l):
    bsz, seq, d = x.shape
    n = w_in.shape[-1]
    tm = min(seq, 1024)
    tn = 2048
    return pl.pallas_call(
        _inproj_kernel,
        out_shape=jax.ShapeDtypeStruct((bsz, seq, n), BF16),
        grid=(bsz, seq // tm, n // tn),
        in_specs=[
            pl.BlockSpec((1, tm, d), lambda b, i, j: (b, i, 0)),
            pl.BlockSpec((1, 1, 6, d), lambda b, i, j: (l, b, 0, 0)),
            pl.BlockSpec((1, 1, d), lambda b, i, j: (l, 0, 0)),
            pl.BlockSpec((1, d, tn), lambda b, i, j: (l, 0, j)),
        ],
        out_specs=pl.BlockSpec((1, tm, tn), lambda b, i, j: (b, i, j)),
        scratch_shapes=[pltpu.VMEM((tm, d), BF16)],
        compiler_params=_cparams(("parallel", "parallel", "arbitrary")),
        name="in_proj",
    )(x, mod, gain, w_in)


def _ret_kernel(q_ref, k_ref, v_ref, g_ref, cc_ref, ss_ref, o_ref, state_ref):
    head = pl.program_id(1)

    @pl.when(pl.program_id(2) == 0)
    def _():
        state_ref[...] = jnp.zeros_like(state_ref)

    hv = jnp.zeros((1, 1), F32) + head.astype(F32)
    log_gamma = jnp.log1p(-jnp.exp2(-5.0 - hv))
    ii = lax.broadcasted_iota(jnp.int32, (CHUNK, CHUNK), 0)
    jj = lax.broadcasted_iota(jnp.int32, (CHUNK, CHUNK), 1)
    causal = ii >= jj
    diff = jnp.where(causal, (ii - jj).astype(F32), 0.0)
    decay_intra = jnp.where(causal, jnp.exp(log_gamma * diff), 0.0)
    pos = lax.broadcasted_iota(jnp.int32, (CHUNK, 1), 0).astype(F32)
    decay_q = jnp.exp(log_gamma * (pos + 1.0))
    decay_k = jnp.exp(log_gamma * (CHUNK - 1.0 - pos))
    decay_chunk = jnp.exp(log_gamma * CHUNK)

    cc = cc_ref[0]
    ss = ss_ref[0]
    q = q_ref[0].astype(F32)
    k = k_ref[0].astype(F32)
    q = q * cc + pltpu.roll(q, RET_DK // 2, 1) * ss
    k = (k * cc + pltpu.roll(k, RET_DK // 2, 1) * ss) * RET_DK ** -0.5

    ts = q.shape[0]
    nc = ts // CHUNK
    qb = q.astype(BF16)
    kb = k.astype(BF16)
    qd = (q.reshape(nc, CHUNK, RET_DK) * decay_q).astype(BF16)
    kd = (k.reshape(nc, CHUNK, RET_DK) * decay_k).astype(BF16)
    v = v_ref[0]
    chunks = [slice(c * CHUNK, (c + 1) * CHUNK) for c in range(nc)]
    scores = [_dot_nt(qb[sl], kb[sl]) for sl in chunks]
    kv = [_dot_tn(kd[c], v[sl]) for c, sl in enumerate(chunks)]
    scores = [(s * decay_intra).astype(BF16) for s in scores]
    intra = [_dot(scores[c], v[sl]) for c, sl in enumerate(chunks)]
    state = state_ref[...]
    states = []
    for c in range(nc):
        states.append(state.astype(BF16))
        state = decay_chunk * state + kv[c]
    state_ref[...] = state
    inter = [_dot(qd[c], states[c]) for c in range(nc)]
    out = jnp.concatenate(intra, axis=0) + jnp.concatenate(inter, axis=0)
    out = out * lax.rsqrt(jnp.mean(out * out, axis=-1, keepdims=True) + EPS)
    o_ref[0] = (out * _silu(g_ref[0].astype(F32))).astype(BF16)


def _retention(proj, cc, ss):
    bsz, seq, _ = proj.shape
    ts = min(seq, 512)
    qb, kb = OFF_RQ // RET_DK, OFF_RK // RET_DK
    vb, gb = OFF_RV // RET_DV, OFF_RG // RET_DV
    return pl.pallas_call(
        _ret_kernel,
        out_shape=jax.ShapeDtypeStruct((bsz, seq, RET_HEADS * RET_DV), BF16),
        grid=(bsz, RET_HEADS, seq // ts),
        in_specs=[
            pl.BlockSpec((1, ts, RET_DK), lambda b, h, t: (b, t, qb + h)),
            pl.BlockSpec((1, ts, RET_DK), lambda b, h, t: (b, t, kb + h)),
            pl.BlockSpec((1, ts, RET_DV), lambda b, h, t: (b, t, vb + h)),
            pl.BlockSpec((1, ts, RET_DV), lambda b, h, t: (b, t, gb + h)),
            pl.BlockSpec((1, ts, RET_DK), lambda b, h, t: (b, t, 0)),
            pl.BlockSpec((1, ts, RET_DK), lambda b, h, t: (b, t, 0)),
        ],
        out_specs=pl.BlockSpec((1, ts, RET_DV), lambda b, h, t: (b, t, h)),
        scratch_shapes=[pltpu.VMEM((RET_DK, RET_DV), F32)],
        compiler_params=_cparams(("parallel", "parallel", "arbitrary")),
        name="retention",
    )(proj, proj, proj, proj, cc, ss)


def _causal_conv(xs_ref, rows, w, b):
    y = b
    for k in range(CONV_W):
        s = CONV_W - 1 - k
        y = y + w[k:k + 1, :] * xs_ref[pl.ds(V7X_SUBLANES - s, rows), :]
    return y


def _neg_expm1(z, ez):
    return jnp.where(ez == 1.0, -z, (1.0 - ez) * z / jnp.log(ez))


def _lru_kernel(lx_ref, ly_ref, cw_ref, cb_ref, wr_ref, br_ref, wi_ref, bi_ref, lam_ref, o_ref,
                xs_ref, al_ref, ul_ref, carry_ref):
    seq, width = lx_ref.shape[1:]
    groups = seq // V7X_SUBLANES
    xs_ref[pl.ds(0, V7X_SUBLANES), :] = jnp.zeros((V7X_SUBLANES, width), F32)
    xs_ref[pl.ds(V7X_SUBLANES, seq), :] = lx_ref[0].astype(F32)
    xl = _causal_conv(xs_ref, seq, cw_ref[0], cb_ref[0])
    xb = xl.astype(BF16)
    r = _sigmoid(_dot(xb, wr_ref[0, 0]) + br_ref[0])
    i = _sigmoid(_dot(xb, wi_ref[0, 0]) + bi_ref[0])
    lam = lam_ref[0]
    softplus_neg_lam = jnp.maximum(-lam, 0.0) + jnp.log1p(jnp.exp(-jnp.abs(lam)))
    log_a = r * (-LRU_C * softplus_neg_lam)
    a = jnp.exp(log_a)
    y = _neg_expm1(r * (-2.0 * LRU_C * softplus_neg_lam), a * a)
    u = jnp.where(y > 0.0, y * lax.rsqrt(y), 0.0) * (i * xl)

    a = a.reshape(groups, V7X_SUBLANES, width)
    u = u.reshape(groups, V7X_SUBLANES, width)
    r8 = lax.broadcasted_iota(jnp.int32, a.shape, 1)
    s = 1
    while s < V7X_SUBLANES:
        m = r8 >= s
        a_sh = pltpu.roll(a, s, 1)
        u_sh = pltpu.roll(u, s, 1)
        u = jnp.where(m, a * u_sh + u, u)
        a = jnp.where(m, a * a_sh, a)
        s *= 2

    al_ref[...] = jnp.broadcast_to(a[:, V7X_SUBLANES - 1:, :], a.shape)
    ul_ref[...] = jnp.broadcast_to(u[:, V7X_SUBLANES - 1:, :], u.shape)

    def body(g, carry):
        carry_ref[g] = carry
        return al_ref[g] * carry + ul_ref[g]

    lax.fori_loop(0, groups, body, jnp.zeros((V7X_SUBLANES, width), F32), unroll=8)
    h = (a * carry_ref[...] + u).reshape(seq, width)
    o_ref[0] = (h * jax.nn.gelu(ly_ref[0].astype(F32))).astype(BF16)


def _lru(proj, conv_w, conv_b, w_r, b_r, w_i, b_i, lam, l):
    bsz, seq, _ = proj.shape
    w = V7X_LANES
    nblk = D_MODEL // w
    xb, yb = OFF_LX // w, OFF_LY // w
    groups = seq // V7X_SUBLANES
    vec = lambda: pl.BlockSpec((1, 1, w), lambda b, j: (l, 0, j))
    return pl.pallas_call(
        _lru_kernel,
        out_shape=jax.ShapeDtypeStruct((bsz, seq, D_MODEL), BF16),
        grid=(bsz, nblk),
        in_specs=[
            pl.BlockSpec((1, seq, w), lambda b, j: (b, 0, xb + j)),
            pl.BlockSpec((1, seq, w), lambda b, j: (b, 0, yb + j)),
            pl.BlockSpec((1, CONV_W, w), lambda b, j: (l, 0, j)),
            vec(),
            pl.BlockSpec((1, 1, w, w), lambda b, j: (l, j, 0, 0)),
            vec(),
            pl.BlockSpec((1, 1, w, w), lambda b, j: (l, j, 0, 0)),
            vec(),
            vec(),
        ],
        out_specs=pl.BlockSpec((1, seq, w), lambda b, j: (b, 0, j)),
        scratch_shapes=[pltpu.VMEM((seq + V7X_SUBLANES, w), F32)]
                       + [pltpu.VMEM((groups, V7X_SUBLANES, w), F32)] * 3,
        compiler_params=_cparams(("parallel", "parallel")),
        name="rg_lru",
    )(proj, proj, conv_w, conv_b, w_r, b_r, w_i, b_i, lam)


def _mprep_kernel(mx_ref, cw_ref, cb_ref, wq_ref, wk_ref, wv_ref, wif_ref, bif_ref,
                  q_ref, k_ref, v_ref, gate_ref, xs_ref):
    ts = mx_ref.shape[1]

    @pl.when(pl.program_id(1) == 0)
    def _():
        xs_ref[pl.ds(ts, V7X_SUBLANES), :] = jnp.zeros((V7X_SUBLANES, xs_ref.shape[1]), F32)

    mxb = mx_ref[0]
    xs_ref[pl.ds(0, V7X_SUBLANES), :] = xs_ref[pl.ds(ts, V7X_SUBLANES), :]
    xs_ref[pl.ds(V7X_SUBLANES, ts), :] = mxb.astype(F32)
    y = _causal_conv(xs_ref, ts, cw_ref[0], cb_ref[0])
    xc = _silu(y).astype(BF16)
    acc = jnp.zeros((ts, V7X_LANES), F32) + bif_ref[0]
    nh = MLSTM_HEADS
    for h in range(nh):
        cs = slice(h * MLSTM_DH, (h + 1) * MLSTM_DH)
        mq = _dot(xc[:, cs], wq_ref[0, h]).astype(BF16)
        mk = _dot(xc[:, cs], wk_ref[0, h]).astype(BF16)
        mv = _dot(mxb[:, cs], wv_ref[0, h]).astype(BF16)
        q_ref[0, :, cs] = mq
        k_ref[0, :, cs] = mk
        v_ref[0, :, cs] = mv
        acc = acc + _dot(mq, wif_ref[0, h]) + _dot(mk, wif_ref[0, nh + h]) + _dot(mv, wif_ref[0, 2 * nh + h])
    lane = lax.broadcasted_iota(jnp.int32, acc.shape, 1)
    log_f = jnp.minimum(acc, 0.0) - jnp.log1p(jnp.exp(-jnp.abs(acc)))
    gate_ref[0] = jnp.where(lane >= nh, log_f, acc)


def _mlstm_prep(proj, conv_w, conv_b, wq, wk, wv, wif, bif, l):
    bsz, seq, _ = proj.shape
    width = MLSTM_HEADS * MLSTM_DH
    ts = min(seq, 512)
    nh, dh = MLSTM_HEADS, MLSTM_DH
    qkv = jax.ShapeDtypeStruct((bsz, seq, width), BF16)
    wspec = lambda: pl.BlockSpec((1, nh, dh, dh), lambda b, t: (l, 0, 0, 0))
    ospec = lambda: pl.BlockSpec((1, ts, width), lambda b, t: (b, t, 0))
    return pl.pallas_call(
        _mprep_kernel,
        out_shape=(qkv, qkv, qkv, jax.ShapeDtypeStruct((bsz, seq, V7X_LANES), F32)),
        grid=(bsz, seq // ts),
        in_specs=[
            pl.BlockSpec((1, ts, width), lambda b, t: (b, t, OFF_MX // width)),
            pl.BlockSpec((1, CONV_W, width), lambda b, t: (l, 0, 0)),
            pl.BlockSpec((1, 1, width), lambda b, t: (l, 0, 0)),
            wspec(), wspec(), wspec(),
            pl.BlockSpec((1, 3 * nh, dh, V7X_LANES), lambda b, t: (l, 0, 0, 0)),
            pl.BlockSpec((1, 1, V7X_LANES), lambda b, t: (l, 0, 0)),
        ],
        out_specs=(ospec(), ospec(), ospec(),
                   pl.BlockSpec((1, ts, V7X_LANES), lambda b, t: (b, t, 0))),
        scratch_shapes=[pltpu.VMEM((ts + V7X_SUBLANES, width), F32)],
        compiler_params=_cparams(("parallel", "arbitrary")),
        name="mlstm_prep",
    )(proj, conv_w, conv_b, wq, wk, wv, wif, bif)


def _mlstm_kernel(q_ref, k_ref, v_ref, gate_ref, mo_ref, mn_ref, o_ref, c_ref, n_ref, m_ref):
    head = pl.program_id(1)

    @pl.when(pl.program_id(2) == 0)
    def _():
        c_ref[...] = jnp.zeros_like(c_ref)
        n_ref[...] = jnp.zeros_like(n_ref)
        m_ref[...] = jnp.zeros_like(m_ref)

    gates = gate_ref[0]
    ts = gates.shape[0]
    nc = ts // CHUNK
    lane = lax.broadcasted_iota(jnp.int32, gates.shape, 1)
    ic = jnp.sum(jnp.where(lane == head, gates, 0.0), axis=1, keepdims=True).reshape(nc, CHUNK, 1)
    lf = jnp.sum(jnp.where(lane == head + MLSTM_HEADS, gates, 0.0), axis=1, keepdims=True).reshape(nc, CHUNK, 1)

    ii = lax.broadcasted_iota(jnp.int32, (nc, CHUNK, CHUNK), 1)
    jj = lax.broadcasted_iota(jnp.int32, (nc, CHUNK, CHUNK), 2)
    causal = ii >= jj
    diag = ii == jj
    b_row = jnp.sum(jnp.where(ii <= jj, lf, 0.0), axis=1, keepdims=True)
    lf_row = jnp.sum(jnp.where(diag, lf, 0.0), axis=1, keepdims=True)
    ic_row = jnp.sum(jnp.where(diag, ic, 0.0), axis=1, keepdims=True)
    b_col = jnp.sum(jnp.where(causal, lf_row, 0.0), axis=2, keepdims=True)
    dmat = jnp.where(causal, b_col - b_row + ic_row, -jnp.inf)
    row_max = jnp.max(dmat, axis=2, keepdims=True)

    m_s = m_ref[...]
    m_t_list, m_prev_list = [], []
    for c in range(nc):
        m_prev_list.append(m_s)
        m_tc = jnp.maximum(b_col[c] + m_s, row_max[c])
        m_t_list.append(m_tc)
        m_s = m_tc[CHUNK - 1:]
    m_ref[...] = m_s

    q = q_ref[0]
    v = v_ref[0]
    kf = k_ref[0].astype(F32) * MLSTM_DH ** -0.5
    kb = kf.astype(BF16)
    m_t = jnp.stack(m_t_list)
    m_prev = jnp.stack(m_prev_list)
    w_inter = jnp.exp(b_col + m_prev - m_t)
    b_last = b_col[:, CHUNK - 1:, :]
    m_new = m_t[:, CHUNK - 1:, :]
    w_k = jnp.exp(b_last - b_col + ic - m_new)
    decay = jnp.exp(b_last + m_prev - m_new)
    p = jnp.exp(dmat - m_t)
    kw = kf.reshape(nc, CHUNK, MLSTM_DH) * w_k
    n_add = jnp.sum(kw, axis=1, keepdims=True)
    kwb = kw.astype(BF16)

    chunks = [slice(c * CHUNK, (c + 1) * CHUNK) for c in range(nc)]
    qk = [_dot_nt(q[sl], kb[sl]) for sl in chunks]
    kv = [_dot_tn(kwb[c], v[sl]) for c, sl in enumerate(chunks)]
    s = [qk[c] * p[c] for c in range(nc)]
    intra = [_dot(s[c].astype(BF16), v[sl]) for c, sl in enumerate(chunks)]
    c_s = c_ref[...]
    n_s = n_ref[...]
    c_states, n_states = [], []
    for c in range(nc):
        c_states.append(c_s.astype(BF16))
        n_states.append(n_s)
        c_s = decay[c] * c_s + kv[c]
        n_s = decay[c] * n_s + n_add[c]
    c_ref[...] = c_s
    n_ref[...] = n_s
    inter = [_dot(q[sl], c_states[c]) for c, sl in enumerate(chunks)]
    w_inter = w_inter.reshape(ts, 1)
    num = jnp.concatenate(intra, axis=0) + w_inter * jnp.concatenate(inter, axis=0)
    s_sum = jnp.sum(jnp.stack(s), axis=2, keepdims=True).reshape(ts, 1)
    qn = jnp.sum(q.astype(F32).reshape(nc, CHUNK, MLSTM_DH) * jnp.stack(n_states), axis=2, keepdims=True)
    den = s_sum + w_inter * qn.reshape(ts, 1)
    h = num / jnp.maximum(jnp.abs(den), jnp.exp(-m_t.reshape(ts, 1)))
    o = _sigmoid(mo_ref[0].astype(F32)) * h
    o = o * lax.rsqrt(jnp.mean(o * o, axis=-1, keepdims=True) + EPS)
    o_ref[0] = (o * mn_ref[0]).astype(BF16)


def _mlstm(mq, mk, mv, gates, proj, m_norm, l):
    bsz, seq, width = mq.shape
    dh = MLSTM_DH
    ts = min(seq, 512)
    hspec = lambda: pl.BlockSpec((1, ts, dh), lambda b, h, t: (b, t, h))
    return pl.pallas_call(
        _mlstm_kernel,
        out_shape=jax.ShapeDtypeStruct((bsz, seq, width), BF16),
        grid=(bsz, MLSTM_HEADS, seq // ts),
        in_specs=[
            hspec(), hspec(), hspec(),
            pl.BlockSpec((1, ts, V7X_LANES), lambda b, h, t: (b, t, 0)),
            pl.BlockSpec((1, ts, dh), lambda b, h, t: (b, t, OFF_MO // dh + h)),
            pl.BlockSpec((1, 1, dh), lambda b, h, t: (l, 0, h)),
        ],
        out_specs=hspec(),
        scratch_shapes=[pltpu.VMEM((dh, dh), F32), pltpu.VMEM((1, dh), F32), pltpu.VMEM((1, 1), F32)],
        compiler_params=_cparams(("parallel", "parallel", "arbitrary")),
        name="mlstm",
    )(mq, mk, mv, gates, proj, m_norm)


def _merge_kernel(x_ref, ret_ref, lru_ref, mls_ref, g0_ref, g1_ref, g2_ref, mod_ref,
                  wr_ref, wl_ref, wm_ref, wo_ref, o_ref):
    merged = (_sigmoid(g0_ref[0].astype(F32)) * _dot(ret_ref[0], wr_ref[0])
              + _sigmoid(g1_ref[0].astype(F32)) * _dot(lru_ref[0], wl_ref[0])
              + _sigmoid(g2_ref[0].astype(F32)) * _dot(mls_ref[0], wm_ref[0]))
    y = _dot(merged.astype(BF16), wo_ref[0])
    o_ref[0] = x_ref[0] + mod_ref[0, 0, 2:3, :] * y


def _merge(x, ret, lru, mls, proj, mod, w_br_ret, w_br_lru, w_br_mlstm, w_out, l):
    bsz, seq, d = x.shape
    tm = min(seq, 512)
    gb = OFF_GATE // d
    tok = lambda: pl.BlockSpec((1, tm, d), lambda b, i: (b, i, 0))
    wsp = lambda: pl.BlockSpec((1, d, d), lambda b, i: (l, 0, 0))
    return pl.pallas_call(
        _merge_kernel,
        out_shape=jax.ShapeDtypeStruct((bsz, seq, d), F32),
        grid=(bsz, seq // tm),
        in_specs=[
            tok(), tok(), tok(), tok(),
            pl.BlockSpec((1, tm, d), lambda b, i: (b, i, gb)),
            pl.BlockSpec((1, tm, d), lambda b, i: (b, i, gb + 1)),
            pl.BlockSpec((1, tm, d), lambda b, i: (b, i, gb + 2)),
            pl.BlockSpec((1, 1, 6, d), lambda b, i: (l, b, 0, 0)),
            wsp(), wsp(), wsp(), wsp(),
        ],
        out_specs=tok(),
        compiler_params=_cparams(("parallel", "parallel")),
        name="merge_out",
    )(x, ret, lru, mls, proj, proj, proj, mod, w_br_ret, w_br_lru, w_br_mlstm, w_out)


def _ffn_kernel(x_ref, mod_ref, gain_ref, w1_ref, w2_ref, o_ref):
    x = x_ref[0]
    h = _modulated_norm(x, gain_ref[0], mod_ref[0, 0, 4:5, :], mod_ref[0, 0, 3:4, :])
    a = jnp.square(jnp.maximum(_dot(h.astype(BF16), w1_ref[0]), 0.0))
    o_ref[0] = x + mod_ref[0, 0, 5:6, :] * _dot(a.astype(BF16), w2_ref[0])


def _ffn(x, mod, gain, w1, w2, l):
    bsz, seq, d = x.shape
    dff = w1.shape[-1]
    tm = min(seq, 512)
    tok = lambda: pl.BlockSpec((1, tm, d), lambda b, i: (b, i, 0))
    return pl.pallas_call(
        _ffn_kernel,
        out_shape=jax.ShapeDtypeStruct((bsz, seq, d), F32),
        grid=(bsz, seq // tm),
        in_specs=[
            tok(),
            pl.BlockSpec((1, 1, 6, d), lambda b, i: (l, b, 0, 0)),
            pl.BlockSpec((1, 1, d), lambda b, i: (l, 0, 0)),
            pl.BlockSpec((1, d, dff), lambda b, i: (l, 0, 0), pipeline_mode=pl.Buffered(1)),
            pl.BlockSpec((1, dff, d), lambda b, i: (l, 0, 0), pipeline_mode=pl.Buffered(1)),
        ],
        out_specs=tok(),
        compiler_params=_cparams(("parallel", "parallel")),
        name="ffn",
    )(x, mod, gain, w1, w2)


def _final_norm_kernel(x_ref, gain_ref, o_ref):
    x = x_ref[0]
    o_ref[0] = x * lax.rsqrt(jnp.mean(x * x, axis=-1, keepdims=True) + EPS) * gain_ref[...]


def _final_norm(x, gain):
    bsz, seq, d = x.shape
    tm = min(seq, 1024)
    tok = pl.BlockSpec((1, tm, d), lambda b, i: (b, i, 0))
    return pl.pallas_call(
        _final_norm_kernel,
        out_shape=jax.ShapeDtypeStruct((bsz, seq, d), F32),
        grid=(bsz, seq // tm),
        in_specs=[tok, pl.BlockSpec((1, d), lambda b, i: (0, 0))],
        out_specs=tok,
        compiler_params=_cparams(("parallel", "parallel")),
        name="final_norm",
    )(x, gain.reshape(1, d))


def _block_diag_tiles(w, tile):
    depth, nb, bs, _ = w.shape
    rows = nb * bs
    sel = jnp.tile(jnp.eye(bs, dtype=w.dtype), (1, tile // bs))
    dense = jnp.einsum('lre,ec->lrc', w.reshape(depth, rows, bs), sel, precision=lax.Precision.HIGHEST)
    r = (np.arange(rows) % tile) // bs
    c = np.arange(tile) // bs
    dense = jnp.where(jnp.asarray(r[:, None] == c[None, :]), dense, 0.0)
    return dense.reshape(depth, rows // tile, tile, tile)


def kernel(x, c, positions, w_ada, b_ada, norm_mix, norm_mlp, w_in, lru_conv_w, lru_conv_b, lru_w_r, lru_b_r, lru_w_i, lru_b_i, lru_lambda, m_conv_w, m_conv_b, m_w_q, m_w_k, m_w_v, m_w_if, m_b_if, m_norm, w_br_ret, w_br_lru, w_br_mlstm, w_out, w_ff1, w_ff2, final_norm):
    depth = w_in.shape[0]
    bsz, seq, d = x.shape
    nh, dh = MLSTM_HEADS, MLSTM_DH

    mod = _ada(c, w_ada, b_ada).reshape(depth, bsz, 6, d)
    cc, ss = _rope_tables(positions)

    vec = lambda a: a.reshape(depth, 1, a.shape[-1])
    w_in_b = w_in.astype(BF16)
    w_r_t = _block_diag_tiles(lru_w_r, V7X_LANES).astype(BF16)
    w_i_t = _block_diag_tiles(lru_w_i, V7X_LANES).astype(BF16)
    wq_t = _block_diag_tiles(m_w_q, dh).astype(BF16)
    wk_t = _block_diag_tiles(m_w_k, dh).astype(BF16)
    wv_t = _block_diag_tiles(m_w_v, dh).astype(BF16)
    wif_t = jnp.pad(m_w_if, ((0, 0), (0, 0), (0, V7X_LANES - 2 * nh))).reshape(depth, 3 * nh, dh, V7X_LANES).astype(BF16)
    bif_t = jnp.pad(m_b_if, ((0, 0), (0, V7X_LANES - 2 * nh))).reshape(depth, 1, V7X_LANES)
    w_br_ret_b, w_br_lru_b, w_br_mls_b = w_br_ret.astype(BF16), w_br_lru.astype(BF16), w_br_mlstm.astype(BF16)
    w_out_b, w_ff1_b, w_ff2_b = w_out.astype(BF16), w_ff1.astype(BF16), w_ff2.astype(BF16)
    g_mix, g_mlp = vec(norm_mix), vec(norm_mlp)
    l_cb, l_br, l_bi, l_lam = vec(lru_conv_b), vec(lru_b_r), vec(lru_b_i), vec(lru_lambda)
    m_cb, m_nrm = vec(m_conv_b), vec(m_norm)

    for l in range(depth):
        proj = _inproj(x, mod, g_mix, w_in_b, l)
        ret = _retention(proj, cc, ss)
        lru = _lru(proj, lru_conv_w, l_cb, w_r_t, l_br, w_i_t, l_bi, l_lam, l)
        mq, mk, mv, gates = _mlstm_prep(proj, m_conv_w, m_cb, wq_t, wk_t, wv_t, wif_t, bif_t, l)
        mls = _mlstm(mq, mk, mv, gates, proj, m_nrm, l)
        x = _merge(x, ret, lru, mls, proj, mod, w_br_ret_b, w_br_lru_b, w_br_mls_b, w_out_b, l)
        x = _ffn(x, mod, g_mlp, w_ff1_b, w_ff2_b, l)
    return _final_norm(x, final_norm)
```

```python
import functools

import jax
import jax.numpy as jnp
import numpy as np
from jax import lax
from jax.experimental import pallas as pl
from jax.experimental.pallas import tpu as pltpu

F32 = jnp.float32
BF16 = jnp.bfloat16

D_MODEL = 1024
RET_HEADS = 4
RET_DK = 128
RET_DV = 256
ROPE_BASE = 10000.0
LRU_BLOCK = 64
LRU_C = 8.0
CONV_W = 4
MLSTM_HEADS = 4
MLSTM_DH = 256
QKV_BLOCK = 4
CHUNK = 128
D_FF = 4 * D_MODEL
EPS = 1e-6

OFF_RQ, OFF_RK, OFF_RV, OFF_RG = 0, 512, 1024, 2048
OFF_LX, OFF_LY, OFF_MX, OFF_MO, OFF_GATE = 3072, 4096, 5120, 6144, 7168
D_IN = 10240

V7X_LANES = 128
V7X_SUBLANES = 8
V7X_VMEM_LIMIT_BYTES = 56 * 1024 * 1024


def _cparams(sem):
    return pltpu.CompilerParams(dimension_semantics=sem, vmem_limit_bytes=V7X_VMEM_LIMIT_BYTES)


def _dot(a, b):
    return jnp.dot(a, b, preferred_element_type=F32)


def _dot_nt(a, b):
    return lax.dot_general(a, b, (((1,), (1,)), ((), ())), preferred_element_type=F32)


def _dot_tn(a, b):
    return lax.dot_general(a, b, (((0,), (0,)), ((), ())), preferred_element_type=F32)


def _sigmoid(x):
    return 1.0 / (1.0 + jnp.exp(-x))


def _silu(x):
    return x * _sigmoid(x)


def _ada_kernel(c_ref, w_ref, b_ref, o_ref):
    cond = _silu(c_ref[...])
    o_ref[0] = _dot(cond.astype(BF16), w_ref[0].astype(BF16)) + b_ref[0]


def _ada(c, w_ada, b_ada):
    depth, d, n = w_ada.shape
    bsz = c.shape[0]
    tn = 1024
    return pl.pallas_call(
        _ada_kernel,
        out_shape=jax.ShapeDtypeStruct((depth, bsz, n), F32),
        grid=(depth, n // tn),
        in_specs=[
            pl.BlockSpec((bsz, d), lambda l, j: (0, 0)),
            pl.BlockSpec((1, d, tn), lambda l, j: (l, 0, j)),
            pl.BlockSpec((1, 1, tn), lambda l, j: (l, 0, j)),
        ],
        out_specs=pl.BlockSpec((1, bsz, tn), lambda l, j: (l, 0, j)),
        compiler_params=_cparams(("parallel", "parallel")),
        name="ada_mod",
    )(c, w_ada, b_ada.reshape(depth, 1, n))


def _rope_kernel(pos_ref, invf_ref, cc_ref, ss_ref):
    ang = pos_ref[0] * invf_ref[...]
    lane = lax.broadcasted_iota(jnp.int32, ang.shape, 1)
    sn = jnp.sin(ang)
    cc_ref[0] = jnp.cos(ang)
    ss_ref[0] = jnp.where(lane < RET_DK // 2, -sn, sn)


def _rope_tables(positions):
    bsz, seq = positions.shape
    half = RET_DK // 2
    inv_freq = ROPE_BASE ** (-jnp.arange(half, dtype=F32) / half)
    invf = jnp.concatenate([inv_freq, inv_freq]).reshape(1, RET_DK)
    pos = positions.astype(F32).reshape(bsz, seq, 1)
    ts = min(seq, 512)
    out = jax.ShapeDtypeStruct((bsz, seq, RET_DK), F32)
    return pl.pallas_call(
        _rope_kernel,
        out_shape=(out, out),
        grid=(bsz, seq // ts),
        in_specs=[
            pl.BlockSpec((1, ts, 1), lambda b, t: (b, t, 0)),
            pl.BlockSpec((1, RET_DK), lambda b, t: (0, 0)),
        ],
        out_specs=(pl.BlockSpec((1, ts, RET_DK), lambda b, t: (b, t, 0)),
                   pl.BlockSpec((1, ts, RET_DK), lambda b, t: (b, t, 0))),
        compiler_params=_cparams(("parallel", "parallel")),
        name="rope_tables",
    )(pos, invf)


def _modulated_norm(x, gain, scale, shift):
    xn = x * lax.rsqrt(jnp.mean(x * x, axis=-1, keepdims=True) + EPS)
    return xn * gain * (1.0 + scale) + shift


def _inproj_kernel(x_ref, mod_ref, gain_ref, w_ref, o_ref, h_ref):
    @pl.when(pl.program_id(2) == 0)
    def _():
        h = _modulated_norm(x_ref[0], gain_ref[0], mod_ref[0, 0, 1:2, :], mod_ref[0, 0, 0:1, :])
        h_ref[...] = h.astype(BF16)

    o_ref[0] = _dot(h_ref[...], w_ref[0]).astype(BF16)


def _inproj(x, mod, gain, w_in, l):
    bsz, seq, d = x.shape
    n = w_in.shape[-1]
    tm = min(seq, 1024)
    tn = 2048
    return pl.pallas_call(
        _inproj_kernel,
        out_shape=jax.ShapeDtypeStruct((bsz, seq, n), BF16),
        grid=(bsz, seq // tm, n // tn),
        in_specs=[
            pl.BlockSpec((1, tm, d), lambda b, i, j: (b, i, 0)),
            pl.BlockSpec((1, 1, 6, d), lambda b, i, j: (l, b, 0, 0)),
            pl.BlockSpec((1, 1, d), lambda b, i, j: (l, 0, 0)),
            pl.BlockSpec((1, d, tn), lambda b, i, j: (l, 0, j)),
        ],
        out_specs=pl.BlockSpec((1, tm, tn), lambda b, i, j: (b, i, j)),
        scratch_shapes=[pltpu.VMEM((tm, d), BF16)],
        compiler_params=_cparams(("parallel", "parallel", "arbitrary")),
        name="in_proj",
    )(x, mod, gain, w_in)


def _ret_kernel(q_ref, k_ref, v_ref, g_ref, cc_ref, ss_ref, o_ref, state_ref):
    head = pl.program_id(1)

    @pl.when(pl.program_id(2) == 0)
    def _():
        state_ref[...] = jnp.zeros_like(state_ref)

    hv = jnp.zeros((1, 1), F32) + head.astype(F32)
    log_gamma = jnp.log1p(-jnp.exp2(-5.0 - hv))
    ii = lax.broadcasted_iota(jnp.int32, (CHUNK, CHUNK), 0)
    jj = lax.broadcasted_iota(jnp.int32, (CHUNK, CHUNK), 1)
    causal = ii >= jj
    diff = jnp.where(causal, (ii - jj).astype(F32), 0.0)
    decay_intra = jnp.where(causal, jnp.exp(log_gamma * diff), 0.0)
    pos = lax.broadcasted_iota(jnp.int32, (CHUNK, 1), 0).astype(F32)
    decay_q = jnp.exp(log_gamma * (pos + 1.0))
    decay_k = jnp.exp(log_gamma * (CHUNK - 1.0 - pos))
    decay_chunk = jnp.exp(log_gamma * CHUNK)

    cc = cc_ref[0]
    ss = ss_ref[0]
    q = q_ref[0].astype(F32)
    k = k_ref[0].astype(F32)
    q = q * cc + pltpu.roll(q, RET_DK // 2, 1) * ss
    k = (k * cc + pltpu.roll(k, RET_DK // 2, 1) * ss) * RET_DK ** -0.5

    ts = q.shape[0]
    nc = ts // CHUNK
    qb = q.astype(BF16)
    kb = k.astype(BF16)
    qd = (q.reshape(nc, CHUNK, RET_DK) * decay_q).astype(BF16)
    kd = (k.reshape(nc, CHUNK, RET_DK) * decay_k).astype(BF16)
    v = v_ref[0]
    chunks = [slice(c * CHUNK, (c + 1) * CHUNK) for c in range(nc)]
    scores = [_dot_nt(qb[sl], kb[sl]) for sl in chunks]
    kv = [_dot_tn(kd[c], v[sl]) for c, sl in enumerate(chunks)]
    scores = [(s * decay_intra).astype(BF16) for s in scores]
    intra = [_dot(scores[c], v[sl]) for c, sl in enumerate(chunks)]
    state = state_ref[...]
    states = []
    for c in range(nc):
        states.append(state.astype(BF16))
        state = decay_chunk * state + kv[c]
    state_ref[...] = state
    inter = [_dot(qd[c], states[c]) for c in range(nc)]
    out = jnp.concatenate(intra, axis=0) + jnp.concatenate(inter, axis=0)
    out = out * lax.rsqrt(jnp.mean(out * out, axis=-1, keepdims=True) + EPS)
    o_ref[0] = (out * _silu(g_ref[0].astype(F32))).astype(BF16)


def _retention(proj, cc, ss):
    bsz, seq, _ = proj.shape
    ts = min(seq, 2048)
    qb, kb = OFF_RQ // RET_DK, OFF_RK // RET_DK
    vb, gb = OFF_RV // RET_DV, OFF_RG // RET_DV
    return pl.pallas_call(
        _ret_kernel,
        out_shape=jax.ShapeDtypeStruct((bsz, seq, RET_HEADS * RET_DV), BF16),
        grid=(bsz, RET_HEADS, seq // ts),
        in_specs=[
            pl.BlockSpec((1, ts, RET_DK), lambda b, h, t: (b, t, qb + h)),
            pl.BlockSpec((1, ts, RET_DK), lambda b, h, t: (b, t, kb + h)),
            pl.BlockSpec((1, ts, RET_DV), lambda b, h, t: (b, t, vb + h)),
            pl.BlockSpec((1, ts, RET_DV), lambda b, h, t: (b, t, gb + h)),
            pl.BlockSpec((1, ts, RET_DK), lambda b, h, t: (b, t, 0)),
            pl.BlockSpec((1, ts, RET_DK), lambda b, h, t: (b, t, 0)),
        ],
        out_specs=pl.BlockSpec((1, ts, RET_DV), lambda b, h, t: (b, t, h)),
        scratch_shapes=[pltpu.VMEM((RET_DK, RET_DV), F32)],
        compiler_params=_cparams(("parallel", "parallel", "arbitrary")),
        name="retention",
    )(proj, proj, proj, proj, cc, ss)


def _causal_conv(xs_ref, rows, w, b):
    y = b
    for k in range(CONV_W):
        s = CONV_W - 1 - k
        y = y + w[k:k + 1, :] * xs_ref[pl.ds(V7X_SUBLANES - s, rows), :]
    return y


def _neg_expm1(z, ez):
    return jnp.where(ez == 1.0, -z, (1.0 - ez) * z / jnp.log(ez))


def _lru_kernel(lx_ref, ly_ref, cw_ref, cb_ref, wr_ref, br_ref, wi_ref, bi_ref, lam_ref, o_ref,
                xs_ref, al_ref, ul_ref, carry_ref):
    seq, width = lx_ref.shape[1:]
    groups = seq // V7X_SUBLANES
    xs_ref[pl.ds(0, V7X_SUBLANES), :] = jnp.zeros((V7X_SUBLANES, width), F32)
    xs_ref[pl.ds(V7X_SUBLANES, seq), :] = lx_ref[0].astype(F32)
    xl = _causal_conv(xs_ref, seq, cw_ref[0], cb_ref[0])
    xb = xl.astype(BF16)
    r = _sigmoid(_dot(xb, wr_ref[0, 0]) + br_ref[0])
    i = _sigmoid(_dot(xb, wi_ref[0, 0]) + bi_ref[0])
    lam = lam_ref[0]
    softplus_neg_lam = jnp.maximum(-lam, 0.0) + jnp.log1p(jnp.exp(-jnp.abs(lam)))
    log_a = r * (-LRU_C * softplus_neg_lam)
    a = jnp.exp(log_a)
    y = _neg_expm1(r * (-2.0 * LRU_C * softplus_neg_lam), a * a)
    u = jnp.where(y > 0.0, y * lax.rsqrt(y), 0.0) * (i * xl)

    a = a.reshape(groups, V7X_SUBLANES, width)
    u = u.reshape(groups, V7X_SUBLANES, width)
    r8 = lax.broadcasted_iota(jnp.int32, a.shape, 1)
    s = 1
    while s < V7X_SUBLANES:
        m = r8 >= s
        a_sh = pltpu.roll(a, s, 1)
        u_sh = pltpu.roll(u, s, 1)
        u = jnp.where(m, a * u_sh + u, u)
        a = jnp.where(m, a * a_sh, a)
        s *= 2

    al_ref[...] = jnp.broadcast_to(a[:, V7X_SUBLANES - 1:, :], a.shape)
    ul_ref[...] = jnp.broadcast_to(u[:, V7X_SUBLANES - 1:, :], u.shape)

    def body(g, carry):
        carry_ref[g] = carry
        return al_ref[g] * carry + ul_ref[g]

    lax.fori_loop(0, groups, body, jnp.zeros((V7X_SUBLANES, width), F32), unroll=8)
    h = (a * carry_ref[...] + u).reshape(seq, width)
    o_ref[0] = (h * jax.nn.gelu(ly_ref[0].astype(F32))).astype(BF16)


def _lru(proj, conv_w, conv_b, w_r, b_r, w_i, b_i, lam, l):
    bsz, seq, _ = proj.shape
    w = V7X_LANES
    nblk = D_MODEL // w
    xb, yb = OFF_LX // w, OFF_LY // w
    groups = seq // V7X_SUBLANES
    vec = lambda: pl.BlockSpec((1, 1, w), lambda b, j: (l, 0, j))
    return pl.pallas_call(
        _lru_kernel,
        out_shape=jax.ShapeDtypeStruct((bsz, seq, D_MODEL), BF16),
        grid=(bsz, nblk),
        in_specs=[
            pl.BlockSpec((1, seq, w), lambda b, j: (b, 0, xb + j)),
            pl.BlockSpec((1, seq, w), lambda b, j: (b, 0, yb + j)),
            pl.BlockSpec((1, CONV_W, w), lambda b, j: (l, 0, j)),
            vec(),
            pl.BlockSpec((1, 1, w, w), lambda b, j: (l, j, 0, 0)),
            vec(),
            pl.BlockSpec((1, 1, w, w), lambda b, j: (l, j, 0, 0)),
            vec(),
            vec(),
        ],
        out_specs=pl.BlockSpec((1, seq, w), lambda b, j: (b, 0, j)),
        scratch_shapes=[pltpu.VMEM((seq + V7X_SUBLANES, w), F32)]
                       + [pltpu.VMEM((groups, V7X_SUBLANES, w), F32)] * 3,
        compiler_params=_cparams(("parallel", "parallel")),
        name="rg_lru",
    )(proj, proj, conv_w, conv_b, w_r, b_r, w_i, b_i, lam)


def _mprep_kernel(mx_ref, cw_ref, cb_ref, wq_ref, wk_ref, wv_ref, wif_ref, bif_ref,
                  q_ref, k_ref, v_ref, gate_ref, xs_ref):
    ts = mx_ref.shape[1]

    @pl.when(pl.program_id(1) == 0)
    def _():
        xs_ref[pl.ds(ts, V7X_SUBLANES), :] = jnp.zeros((V7X_SUBLANES, xs_ref.shape[1]), F32)

    mxb = mx_ref[0]
    xs_ref[pl.ds(0, V7X_SUBLANES), :] = xs_ref[pl.ds(ts, V7X_SUBLANES), :]
    xs_ref[pl.ds(V7X_SUBLANES, ts), :] = mxb.astype(F32)
    y = _causal_conv(xs_ref, ts, cw_ref[0], cb_ref[0])
    xc = _silu(y).astype(BF16)
    acc = jnp.zeros((ts, V7X_LANES), F32) + bif_ref[0]
    nh = MLSTM_HEADS
    for h in range(nh):
        cs = slice(h * MLSTM_DH, (h + 1) * MLSTM_DH)
        mq = _dot(xc[:, cs], wq_ref[0, h]).astype(BF16)
        mk = _dot(xc[:, cs], wk_ref[0, h]).astype(BF16)
        mv = _dot(mxb[:, cs], wv_ref[0, h]).astype(BF16)
        q_ref[0, :, cs] = mq
        k_ref[0, :, cs] = mk
        v_ref[0, :, cs] = mv
        acc = acc + _dot(mq, wif_ref[0, h]) + _dot(mk, wif_ref[0, nh + h]) + _dot(mv, wif_ref[0, 2 * nh + h])
    lane = lax.broadcasted_iota(jnp.int32, acc.shape, 1)
    log_f = jnp.minimum(acc, 0.0) - jnp.log1p(jnp.exp(-jnp.abs(acc)))
    gate_ref[0] = jnp.where(lane >= nh, log_f, acc)


def _mlstm_prep(proj, conv_w, conv_b, wq, wk, wv, wif, bif, l):
    bsz, seq, _ = proj.shape
    width = MLSTM_HEADS * MLSTM_DH
    ts = min(seq, 512)
    nh, dh = MLSTM_HEADS, MLSTM_DH
    qkv = jax.ShapeDtypeStruct((bsz, seq, width), BF16)
    wspec = lambda: pl.BlockSpec((1, nh, dh, dh), lambda b, t: (l, 0, 0, 0))
    ospec = lambda: pl.BlockSpec((1, ts, width), lambda b, t: (b, t, 0))
    return pl.pallas_call(
        _mprep_kernel,
        out_shape=(qkv, qkv, qkv, jax.ShapeDtypeStruct((bsz, seq, V7X_LANES), F32)),
        grid=(bsz, seq // ts),
        in_specs=[
            pl.BlockSpec((1, ts, width), lambda b, t: (b, t, OFF_MX // width)),
            pl.BlockSpec((1, CONV_W, width), lambda b, t: (l, 0, 0)),
            pl.BlockSpec((1, 1, width), lambda b, t: (l, 0, 0)),
            wspec(), wspec(), wspec(),
            pl.BlockSpec((1, 3 * nh, dh, V7X_LANES), lambda b, t: (l, 0, 0, 0)),
            pl.BlockSpec((1, 1, V7X_LANES), lambda b, t: (l, 0, 0)),
        ],
        out_specs=(ospec(), ospec(), ospec(),
                   pl.BlockSpec((1, ts, V7X_LANES), lambda b, t: (b, t, 0))),
        scratch_shapes=[pltpu.VMEM((ts + V7X_SUBLANES, width), F32)],
        compiler_params=_cparams(("parallel", "arbitrary")),
        name="mlstm_prep",
    )(proj, conv_w, conv_b, wq, wk, wv, wif, bif)


def _mlstm_kernel(q_ref, k_ref, v_ref, gate_ref, mo_ref, mn_ref, o_ref, c_ref, n_ref, m_ref):
    head = pl.program_id(1)

    @pl.when(pl.program_id(2) == 0)
    def _():
        c_ref[...] = jnp.zeros_like(c_ref)
        n_ref[...] = jnp.zeros_like(n_ref)
        m_ref[...] = jnp.zeros_like(m_ref)

    gates = gate_ref[0]
    ts = gates.shape[0]
    nc = ts // CHUNK
    lane = lax.broadcasted_iota(jnp.int32, gates.shape, 1)
    ic = jnp.sum(jnp.where(lane == head, gates, 0.0), axis=1, keepdims=True).reshape(nc, CHUNK, 1)
    lf = jnp.sum(jnp.where(lane == head + MLSTM_HEADS, gates, 0.0), axis=1, keepdims=True).reshape(nc, CHUNK, 1)

    ii = lax.broadcasted_iota(jnp.int32, (nc, CHUNK, CHUNK), 1)
    jj = lax.broadcasted_iota(jnp.int32, (nc, CHUNK, CHUNK), 2)
    causal = ii >= jj
    diag = ii == jj
    b_row = jnp.sum(jnp.where(ii <= jj, lf, 0.0), axis=1, keepdims=True)
    lf_row = jnp.sum(jnp.where(diag, lf, 0.0), axis=1, keepdims=True)
    ic_row = jnp.sum(jnp.where(diag, ic, 0.0), axis=1, keepdims=True)
    b_col = jnp.sum(jnp.where(causal, lf_row, 0.0), axis=2, keepdims=True)
    dmat = jnp.where(causal, b_col - b_row + ic_row, -jnp.inf)
    row_max = jnp.max(dmat, axis=2, keepdims=True)

    m_s = m_ref[...]
    m_t_list, m_prev_list = [], []
    for c in range(nc):
        m_prev_list.append(m_s)
        m_tc = jnp.maximum(b_col[c] + m_s, row_max[c])
        m_t_list.append(m_tc)
        m_s = m_tc[CHUNK - 1:]
    m_ref[...] = m_s

    q = q_ref[0]
    v = v_ref[0]
    kb = k_ref[0] * jnp.asarray(MLSTM_DH ** -0.5, BF16)
    m_t = jnp.stack(m_t_list)
    m_prev = jnp.stack(m_prev_list)
    w_inter = jnp.exp(b_col + m_prev - m_t)
    b_last = b_col[:, CHUNK - 1:, :]
    m_new = m_t[:, CHUNK - 1:, :]
    w_k = jnp.exp(b_last - b_col + ic - m_new)
    decay = jnp.exp(b_last + m_prev - m_new)
    p = jnp.exp(dmat - m_t)
    kwb = kb.reshape(nc, CHUNK, MLSTM_DH) * w_k.astype(BF16)
    ones = jnp.ones((V7X_SUBLANES, CHUNK), BF16)

    chunks = [slice(c * CHUNK, (c + 1) * CHUNK) for c in range(nc)]
    qk = [_dot_nt(q[sl], kb[sl]) for sl in chunks]
    kv = [_dot_tn(kwb[c], v[sl]) for c, sl in enumerate(chunks)]
    n_add = [_dot(ones, kwb[c]) for c in range(nc)]
    s = [qk[c] * p[c] for c in range(nc)]
    intra = [_dot(s[c].astype(BF16), v[sl]) for c, sl in enumerate(chunks)]
    c_s = c_ref[...]
    n_s = n_ref[...]
    c_states, n_states = [], []
    for c in range(nc):
        c_states.append(c_s.astype(BF16))
        n_states.append(n_s)
        c_s = decay[c] * c_s + kv[c]
        n_s = decay[c] * n_s + n_add[c][:1]
    c_ref[...] = c_s
    n_ref[...] = n_s
    inter = [_dot(q[sl], c_states[c]) for c, sl in enumerate(chunks)]
    w_inter = w_inter.reshape(ts, 1)
    num = jnp.concatenate(intra, axis=0) + w_inter * jnp.concatenate(inter, axis=0)
    s_sum = jnp.sum(jnp.stack(s), axis=2, keepdims=True).reshape(ts, 1)
    qn = [_dot_nt(q[sl], jnp.broadcast_to(n_states[c], (V7X_SUBLANES, MLSTM_DH)).astype(BF16))[:, :1]
          for c, sl in enumerate(chunks)]
    den = s_sum + w_inter * jnp.concatenate(qn, axis=0)
    h = num / jnp.maximum(jnp.abs(den), jnp.exp(-m_t.reshape(ts, 1)))
    o = _sigmoid(mo_ref[0].astype(F32)) * h
    o = o * lax.rsqrt(jnp.mean(o * o, axis=-1, keepdims=True) + EPS)
    o_ref[0] = (o * mn_ref[0]).astype(BF16)


def _mlstm(mq, mk, mv, gates, proj, m_norm, l):
    bsz, seq, width = mq.shape
    dh = MLSTM_DH
    ts = min(seq, 1024)
    hspec = lambda: pl.BlockSpec((1, ts, dh), lambda b, h, t: (b, t, h))
    return pl.pallas_call(
        _mlstm_kernel,
        out_shape=jax.ShapeDtypeStruct((bsz, seq, width), BF16),
        grid=(bsz, MLSTM_HEADS, seq // ts),
        in_specs=[
            hspec(), hspec(), hspec(),
            pl.BlockSpec((1, ts, V7X_LANES), lambda b, h, t: (b, t, 0)),
            pl.BlockSpec((1, ts, dh), lambda b, h, t: (b, t, OFF_MO // dh + h)),
            pl.BlockSpec((1, 1, dh), lambda b, h, t: (l, 0, h)),
        ],
        out_specs=hspec(),
        scratch_shapes=[pltpu.VMEM((dh, dh), F32), pltpu.VMEM((1, dh), F32), pltpu.VMEM((1, 1), F32)],
        compiler_params=_cparams(("parallel", "parallel", "arbitrary")),
        name="mlstm",
    )(mq, mk, mv, gates, proj, m_norm)


def _merge_kernel(x_ref, ret_ref, lru_ref, mls_ref, g0_ref, g1_ref, g2_ref, mod_ref,
                  wr_ref, wl_ref, wm_ref, wo_ref, o_ref):
    merged = (_sigmoid(g0_ref[0].astype(F32)) * _dot(ret_ref[0], wr_ref[0])
              + _sigmoid(g1_ref[0].astype(F32)) * _dot(lru_ref[0], wl_ref[0])
              + _sigmoid(g2_ref[0].astype(F32)) * _dot(mls_ref[0], wm_ref[0]))
    y = _dot(merged.astype(BF16), wo_ref[0])
    o_ref[0] = x_ref[0] + mod_ref[0, 0, 2:3, :] * y


def _merge(x, ret, lru, mls, proj, mod, w_br_ret, w_br_lru, w_br_mlstm, w_out, l):
    bsz, seq, d = x.shape
    tm = min(seq, 512)
    gb = OFF_GATE // d
    tok = lambda: pl.BlockSpec((1, tm, d), lambda b, i: (b, i, 0))
    wsp = lambda: pl.BlockSpec((1, d, d), lambda b, i: (l, 0, 0))
    return pl.pallas_call(
        _merge_kernel,
        out_shape=jax.ShapeDtypeStruct((bsz, seq, d), F32),
        grid=(bsz, seq // tm),
        in_specs=[
            tok(), tok(), tok(), tok(),
            pl.BlockSpec((1, tm, d), lambda b, i: (b, i, gb)),
            pl.BlockSpec((1, tm, d), lambda b, i: (b, i, gb + 1)),
            pl.BlockSpec((1, tm, d), lambda b, i: (b, i, gb + 2)),
            pl.BlockSpec((1, 1, 6, d), lambda b, i: (l, b, 0, 0)),
            wsp(), wsp(), wsp(), wsp(),
        ],
        out_specs=tok(),
        compiler_params=_cparams(("parallel", "parallel")),
        name="merge_out",
    )(x, ret, lru, mls, proj, proj, proj, mod, w_br_ret, w_br_lru, w_br_mlstm, w_out)


def _ffn_kernel(x_ref, mod_ref, gain_ref, w1_ref, w2_ref, fgain_ref, o_ref, *, final_norm):
    x = x_ref[0]
    h = _modulated_norm(x, gain_ref[0], mod_ref[0, 0, 4:5, :], mod_ref[0, 0, 3:4, :])
    a = jnp.square(jnp.maximum(_dot(h.astype(BF16), w1_ref[0]), 0.0))
    y = x + mod_ref[0, 0, 5:6, :] * _dot(a.astype(BF16), w2_ref[0])
    if final_norm:
        y = y * lax.rsqrt(jnp.mean(y * y, axis=-1, keepdims=True) + EPS) * fgain_ref[...]
    o_ref[0] = y


def _ffn(x, mod, gain, w1, w2, final_gain, l, final_norm):
    bsz, seq, d = x.shape
    dff = w1.shape[-1]
    tm = min(seq, 512)
    tok = lambda: pl.BlockSpec((1, tm, d), lambda b, i: (b, i, 0))
    return pl.pallas_call(
        functools.partial(_ffn_kernel, final_norm=final_norm),
        out_shape=jax.ShapeDtypeStruct((bsz, seq, d), F32),
        grid=(bsz, seq // tm),
        in_specs=[
            tok(),
            pl.BlockSpec((1, 1, 6, d), lambda b, i: (l, b, 0, 0)),
            pl.BlockSpec((1, 1, d), lambda b, i: (l, 0, 0)),
            pl.BlockSpec((1, d, dff), lambda b, i: (l, 0, 0), pipeline_mode=pl.Buffered(1)),
            pl.BlockSpec((1, dff, d), lambda b, i: (l, 0, 0), pipeline_mode=pl.Buffered(1)),
            pl.BlockSpec((1, d), lambda b, i: (0, 0)),
        ],
        out_specs=tok(),
        compiler_params=_cparams(("parallel", "parallel")),
        name="ffn",
    )(x, mod, gain, w1, w2, final_gain)


def _block_diag_tiles(w, tile):
    depth, nb, bs, _ = w.shape
    rows = nb * bs
    sel = jnp.tile(jnp.eye(bs, dtype=w.dtype), (1, tile // bs))
    dense = jnp.einsum('lre,ec->lrc', w.reshape(depth, rows, bs), sel, precision=lax.Precision.HIGHEST)
    r = (np.arange(rows) % tile) // bs
    c = np.arange(tile) // bs
    dense = jnp.where(jnp.asarray(r[:, None] == c[None, :]), dense, 0.0)
    return dense.reshape(depth, rows // tile, tile, tile)


def kernel(x, c, positions, w_ada, b_ada, norm_mix, norm_mlp, w_in, lru_conv_w, lru_conv_b, lru_w_r, lru_b_r, lru_w_i, lru_b_i, lru_lambda, m_conv_w, m_conv_b, m_w_q, m_w_k, m_w_v, m_w_if, m_b_if, m_norm, w_br_ret, w_br_lru, w_br_mlstm, w_out, w_ff1, w_ff2, final_norm):
    depth = w_in.shape[0]
    bsz, seq, d = x.shape
    nh, dh = MLSTM_HEADS, MLSTM_DH

    mod = _ada(c, w_ada, b_ada).reshape(depth, bsz, 6, d)
    cc, ss = _rope_tables(positions)

    vec = lambda a: a.reshape(depth, 1, a.shape[-1])
    w_in_b = w_in.astype(BF16)
    w_r_t = _block_diag_tiles(lru_w_r, V7X_LANES).astype(BF16)
    w_i_t = _block_diag_tiles(lru_w_i, V7X_LANES).astype(BF16)
    wq_t = _block_diag_tiles(m_w_q, dh).astype(BF16)
    wk_t = _block_diag_tiles(m_w_k, dh).astype(BF16)
    wv_t = _block_diag_tiles(m_w_v, dh).astype(BF16)
    wif_t = jnp.pad(m_w_if, ((0, 0), (0, 0), (0, V7X_LANES - 2 * nh))).reshape(depth, 3 * nh, dh, V7X_LANES).astype(BF16)
    bif_t = jnp.pad(m_b_if, ((0, 0), (0, V7X_LANES - 2 * nh))).reshape(depth, 1, V7X_LANES)
    w_br_ret_b, w_br_lru_b, w_br_mls_b = w_br_ret.astype(BF16), w_br_lru.astype(BF16), w_br_mlstm.astype(BF16)
    w_out_b, w_ff1_b, w_ff2_b = w_out.astype(BF16), w_ff1.astype(BF16), w_ff2.astype(BF16)
    g_mix, g_mlp = vec(norm_mix), vec(norm_mlp)
    l_cb, l_br, l_bi, l_lam = vec(lru_conv_b), vec(lru_b_r), vec(lru_b_i), vec(lru_lambda)
    m_cb, m_nrm = vec(m_conv_b), vec(m_norm)
    f_gain = final_norm.reshape(1, d)

    for l in range(depth):
        proj = _inproj(x, mod, g_mix, w_in_b, l)
        ret = _retention(proj, cc, ss)
        lru = _lru(proj, lru_conv_w, l_cb, w_r_t, l_br, w_i_t, l_bi, l_lam, l)
        mq, mk, mv, gates = _mlstm_prep(proj, m_conv_w, m_cb, wq_t, wk_t, wv_t, wif_t, bif_t, l)
        mls = _mlstm(mq, mk, mv, gates, proj, m_nrm, l)
        x = _merge(x, ret, lru, mls, proj, mod, w_br_ret_b, w_br_lru_b, w_br_mls_b, w_out_b, l)
        x = _ffn(x, mod, g_mlp, w_ff1_b, w_ff2_b, f_gain, l, final_norm=(l == depth - 1))
    return x
```

```python
import functools

import jax
import jax.numpy as jnp
import numpy as np
from jax import lax
from jax.experimental import pallas as pl
from jax.experimental.pallas import tpu as pltpu

F32 = jnp.float32
BF16 = jnp.bfloat16

D_MODEL = 1024
RET_HEADS = 4
RET_DK = 128
RET_DV = 256
ROPE_BASE = 10000.0
LRU_C = 8.0
CONV_W = 4
MLSTM_HEADS = 4
MLSTM_DH = 256
CHUNK = 128
EPS = 1e-6

OFF_RQ, OFF_RK, OFF_RV, OFF_RG = 0, 512, 1024, 2048
OFF_LX, OFF_LY, OFF_MX, OFF_MO, OFF_GATE = 3072, 4096, 5120, 6144, 7168

V7X_LANES = 128
V7X_SUBLANES = 8
V7X_VMEM_LIMIT_BYTES = 56 * 1024 * 1024


def _cparams(sem):
    return pltpu.CompilerParams(dimension_semantics=sem, vmem_limit_bytes=V7X_VMEM_LIMIT_BYTES)


def _dot(a, b):
    return jnp.dot(a, b, preferred_element_type=F32)


def _dot_nt(a, b):
    return lax.dot_general(a, b, (((1,), (1,)), ((), ())), preferred_element_type=F32)


def _dot_tn(a, b):
    return lax.dot_general(a, b, (((0,), (0,)), ((), ())), preferred_element_type=F32)


def _sigmoid(x):
    return 0.5 * jnp.tanh(0.5 * x) + 0.5


def _silu(x):
    return x * _sigmoid(x)


def _ada_kernel(c_ref, w_ref, b_ref, o_ref):
    cond = _silu(c_ref[...])
    o_ref[0] = _dot(cond.astype(BF16), w_ref[0].astype(BF16)) + b_ref[0]


def _ada(c, w_ada, b_ada):
    depth, d, n = w_ada.shape
    bsz = c.shape[0]
    tn = 1024
    return pl.pallas_call(
        _ada_kernel,
        out_shape=jax.ShapeDtypeStruct((depth, bsz, n), F32),
        grid=(depth, n // tn),
        in_specs=[
            pl.BlockSpec((bsz, d), lambda l, j: (0, 0)),
            pl.BlockSpec((1, d, tn), lambda l, j: (l, 0, j)),
            pl.BlockSpec((1, 1, tn), lambda l, j: (l, 0, j)),
        ],
        out_specs=pl.BlockSpec((1, bsz, tn), lambda l, j: (l, 0, j)),
        compiler_params=_cparams(("parallel", "parallel")),
        name="ada_mod",
    )(c, w_ada, b_ada.reshape(depth, 1, n))


def _rope_kernel(pos_ref, invf_ref, cc_ref, ss_ref):
    ang = pos_ref[0] * invf_ref[...]
    lane = lax.broadcasted_iota(jnp.int32, ang.shape, 1)
    sn = jnp.sin(ang)
    cc_ref[0] = jnp.cos(ang)
    ss_ref[0] = jnp.where(lane < RET_DK // 2, -sn, sn)


def _rope_tables(positions):
    bsz, seq = positions.shape
    half = RET_DK // 2
    inv_freq = ROPE_BASE ** (-jnp.arange(half, dtype=F32) / half)
    invf = jnp.concatenate([inv_freq, inv_freq]).reshape(1, RET_DK)
    pos = positions.astype(F32).reshape(bsz, seq, 1)
    ts = min(seq, 512)
    out = jax.ShapeDtypeStruct((bsz, seq, RET_DK), F32)
    return pl.pallas_call(
        _rope_kernel,
        out_shape=(out, out),
        grid=(bsz, seq // ts),
        in_specs=[
            pl.BlockSpec((1, ts, 1), lambda b, t: (b, t, 0)),
            pl.BlockSpec((1, RET_DK), lambda b, t: (0, 0)),
        ],
        out_specs=(pl.BlockSpec((1, ts, RET_DK), lambda b, t: (b, t, 0)),
                   pl.BlockSpec((1, ts, RET_DK), lambda b, t: (b, t, 0))),
        compiler_params=_cparams(("parallel", "parallel")),
        name="rope_tables",
    )(pos, invf)


def _modulated_norm(x, gain, scale, shift):
    xn = x * lax.rsqrt(jnp.mean(x * x, axis=-1, keepdims=True) + EPS)
    return xn * gain * (1.0 + scale) + shift


def _prenorm_kernel(x_ref, mod_ref, gain_ref, o_ref):
    h = _modulated_norm(x_ref[0], gain_ref[0], mod_ref[0, 0, 1:2, :], mod_ref[0, 0, 0:1, :])
    o_ref[0] = h.astype(BF16)


def _prenorm(x, mod, gain, l):
    bsz, seq, d = x.shape
    tm = min(seq, 1024)
    tok = lambda: pl.BlockSpec((1, tm, d), lambda b, i: (b, i, 0))
    return pl.pallas_call(
        _prenorm_kernel,
        out_shape=jax.ShapeDtypeStruct((bsz, seq, d), BF16),
        grid=(bsz, seq // tm),
        in_specs=[
            tok(),
            pl.BlockSpec((1, 1, 6, d), lambda b, i: (l, b, 0, 0)),
            pl.BlockSpec((1, 1, d), lambda b, i: (l, 0, 0)),
        ],
        out_specs=tok(),
        compiler_params=_cparams(("parallel", "parallel")),
        name="prenorm",
    )(x, mod, gain)


def _inproj_kernel(h_ref, w_ref, o_ref):
    o_ref[0] = _dot(h_ref[0], w_ref[0]).astype(BF16)


def _inproj(h, w_in, l):
    bsz, seq, d = h.shape
    n = w_in.shape[-1]
    tm = min(seq, 1024)
    tn = 2048
    return pl.pallas_call(
        _inproj_kernel,
        out_shape=jax.ShapeDtypeStruct((bsz, seq, n), BF16),
        grid=(n // tn, bsz, seq // tm),
        in_specs=[
            pl.BlockSpec((1, tm, d), lambda j, b, i: (b, i, 0)),
            pl.BlockSpec((1, d, tn), lambda j, b, i: (l, 0, j)),
        ],
        out_specs=pl.BlockSpec((1, tm, tn), lambda j, b, i: (b, i, j)),
        compiler_params=_cparams(("parallel", "parallel", "parallel")),
        name="in_proj",
    )(h, w_in)


def _ret_kernel(q_ref, k_ref, v_ref, g_ref, cc_ref, ss_ref, o_ref, state_ref):
    head = pl.program_id(1)

    @pl.when(pl.program_id(2) == 0)
    def _():
        state_ref[...] = jnp.zeros_like(state_ref)

    hv = jnp.zeros((1, 1), F32) + head.astype(F32)
    log_gamma = jnp.log1p(-jnp.exp2(-5.0 - hv))
    ii = lax.broadcasted_iota(jnp.int32, (CHUNK, CHUNK), 0)
    jj = lax.broadcasted_iota(jnp.int32, (CHUNK, CHUNK), 1)
    causal = ii >= jj
    diff = jnp.where(causal, (ii - jj).astype(F32), 0.0)
    decay_intra = jnp.where(causal, jnp.exp(log_gamma * diff), 0.0)
    pos = lax.broadcasted_iota(jnp.int32, (CHUNK, 1), 0).astype(F32)
    decay_q = jnp.exp(log_gamma * (pos + 1.0))
    decay_k = jnp.exp(log_gamma * (CHUNK - 1.0 - pos))
    decay_chunk = jnp.exp(log_gamma * CHUNK)

    cc = cc_ref[0]
    ss = ss_ref[0]
    q = q_ref[0].astype(F32)
    k = k_ref[0].astype(F32)
    q = q * cc + pltpu.roll(q, RET_DK // 2, 1) * ss
    k = (k * cc + pltpu.roll(k, RET_DK // 2, 1) * ss) * RET_DK ** -0.5

    ts = q.shape[0]
    nc = ts // CHUNK
    qb = q.astype(BF16)
    kb = k.astype(BF16)
    qd = (q.reshape(nc, CHUNK, RET_DK) * decay_q).astype(BF16)
    kd = (k.reshape(nc, CHUNK, RET_DK) * decay_k).astype(BF16)
    v = v_ref[0]
    chunks = [slice(c * CHUNK, (c + 1) * CHUNK) for c in range(nc)]
    scores = [_dot_nt(qb[sl], kb[sl]) for sl in chunks]
    kv = [_dot_tn(kd[c], v[sl]) for c, sl in enumerate(chunks)]
    scores = [(s * decay_intra).astype(BF16) for s in scores]
    intra = [_dot(scores[c], v[sl]) for c, sl in enumerate(chunks)]
    state = state_ref[...]
    states = []
    for c in range(nc):
        states.append(state.astype(BF16))
        state = decay_chunk * state + kv[c]
    state_ref[...] = state
    inter = [_dot(qd[c], states[c]) for c in range(nc)]
    out = jnp.concatenate(intra, axis=0) + jnp.concatenate(inter, axis=0)
    out = out * lax.rsqrt(jnp.mean(out * out, axis=-1, keepdims=True) + EPS)
    o_ref[0] = (out * _silu(g_ref[0].astype(F32))).astype(BF16)


def _retention(proj, cc, ss):
    bsz, seq, _ = proj.shape
    ts = min(seq, 2048)
    qb, kb = OFF_RQ // RET_DK, OFF_RK // RET_DK
    vb, gb = OFF_RV // RET_DV, OFF_RG // RET_DV
    return pl.pallas_call(
        _ret_kernel,
        out_shape=jax.ShapeDtypeStruct((bsz, seq, RET_HEADS * RET_DV), BF16),
        grid=(bsz, RET_HEADS, seq // ts),
        in_specs=[
            pl.BlockSpec((1, ts, RET_DK), lambda b, h, t: (b, t, qb + h)),
            pl.BlockSpec((1, ts, RET_DK), lambda b, h, t: (b, t, kb + h)),
            pl.BlockSpec((1, ts, RET_DV), lambda b, h, t: (b, t, vb + h)),
            pl.BlockSpec((1, ts, RET_DV), lambda b, h, t: (b, t, gb + h)),
            pl.BlockSpec((1, ts, RET_DK), lambda b, h, t: (b, t, 0)),
            pl.BlockSpec((1, ts, RET_DK), lambda b, h, t: (b, t, 0)),
        ],
        out_specs=pl.BlockSpec((1, ts, RET_DV), lambda b, h, t: (b, t, h)),
        scratch_shapes=[pltpu.VMEM((RET_DK, RET_DV), F32)],
        compiler_params=_cparams(("parallel", "parallel", "arbitrary")),
        name="retention",
    )(proj, proj, proj, proj, cc, ss)


def _causal_conv(xs_ref, rows, w, b):
    y = b
    for k in range(CONV_W):
        s = CONV_W - 1 - k
        y = y + w[k:k + 1, :] * xs_ref[pl.ds(V7X_SUBLANES - s, rows), :]
    return y


LRU_COLS = 256
LRU_SEGS = V7X_SUBLANES


def _lru_pitch(seg_len):
    tiles = seg_len // V7X_SUBLANES
    return (tiles + 1 - tiles % 2) * V7X_SUBLANES


def _lru_kernel(lx_ref, ly_ref, cw_ref, cb_ref, wr_ref, br_ref, wi_ref, bi_ref, lam_ref, o_ref,
                xs_ref, a_ref, u_ref, h_ref, p_ref):
    seq = lx_ref.shape[1]
    nslab = lx_ref.shape[2] // V7X_LANES
    seg_len = seq // LRU_SEGS
    pitch = _lru_pitch(seg_len)
    zeros8 = jnp.zeros((V7X_SUBLANES, V7X_LANES), F32)

    for s in range(nslab):
        cs = slice(s * V7X_LANES, (s + 1) * V7X_LANES)
        xs_ref[s, pl.ds(0, V7X_SUBLANES), :] = zeros8
        xs_ref[s, pl.ds(V7X_SUBLANES, seq), :] = lx_ref[0, :, cs].astype(F32)
        xl = _causal_conv(xs_ref.at[s], seq, cw_ref[0, :, cs], cb_ref[0, :, cs])
        xb = xl.astype(BF16)
        tr = jnp.tanh(0.5 * (_dot(xb, wr_ref[0, s]) + br_ref[0, :, cs]))
        ti = jnp.tanh(0.5 * (_dot(xb, wi_ref[0, s]) + bi_ref[0, :, cs]))
        lam = lam_ref[0, :, cs]
        softplus_neg_lam = jnp.maximum(-lam, 0.0) + jnp.log1p(jnp.exp(-jnp.abs(lam)))
        log_a = (tr + 1.0) * (-0.5 * LRU_C * softplus_neg_lam)
        a = jnp.exp(log_a)
        y = -jnp.tanh(log_a) * (a * a + 1.0)
        u = jnp.where(y > 0.0, y * lax.rsqrt(y), 0.0) * ((0.5 * ti + 0.5) * xl)
        for g in range(LRU_SEGS):
            a_ref[s, pl.ds(g * pitch, seg_len), :] = a[g * seg_len:(g + 1) * seg_len]
            u_ref[s, pl.ds(g * pitch, seg_len), :] = u[g * seg_len:(g + 1) * seg_len]

    def body(t, carry):
        hs, ps = carry
        new_h, new_p = [], []
        for s in range(nslab):
            rows = pl.ds(t, LRU_SEGS, stride=pitch)
            av = a_ref[s, rows, :]
            h = av * hs[s] + u_ref[s, rows, :]
            p = av * ps[s]
            h_ref[s, rows, :] = h
            p_ref[s, rows, :] = p
            new_h.append(h)
            new_p.append(p)
        return tuple(new_h), tuple(new_p)

    init = (tuple(zeros8 for _ in range(nslab)), tuple(zeros8 + 1.0 for _ in range(nslab)))
    h_end, p_end = lax.fori_loop(0, seg_len, body, init, unroll=8)

    for s in range(nslab):
        cs = slice(s * V7X_LANES, (s + 1) * V7X_LANES)
        carry = jnp.zeros((1, V7X_LANES), F32)
        for g in range(LRU_SEGS):
            rows = pl.ds(g * pitch, seg_len)
            h = h_ref[s, rows, :] + p_ref[s, rows, :] * carry
            gate = jax.nn.gelu(ly_ref[0, g * seg_len:(g + 1) * seg_len, cs].astype(F32))
            o_ref[0, g * seg_len:(g + 1) * seg_len, cs] = (h * gate).astype(BF16)
            carry = h_end[s][g:g + 1] + p_end[s][g:g + 1] * carry


def _lru(proj, conv_w, conv_b, w_r, b_r, w_i, b_i, lam, l):
    bsz, seq, _ = proj.shape
    w = LRU_COLS
    nslab = w // V7X_LANES
    scan_rows = LRU_SEGS * _lru_pitch(seq // LRU_SEGS)
    vec = lambda: pl.BlockSpec((1, 1, w), lambda b, j: (l, 0, j))
    wsp = lambda: pl.BlockSpec((1, nslab, V7X_LANES, V7X_LANES), lambda b, j: (l, j, 0, 0))
    return pl.pallas_call(
        _lru_kernel,
        out_shape=jax.ShapeDtypeStruct((bsz, seq, D_MODEL), BF16),
        grid=(bsz, D_MODEL // w),
        in_specs=[
            pl.BlockSpec((1, seq, w), lambda b, j: (b, 0, OFF_LX // w + j)),
            pl.BlockSpec((1, seq, w), lambda b, j: (b, 0, OFF_LY // w + j)),
            pl.BlockSpec((1, CONV_W, w), lambda b, j: (l, 0, j)),
            vec(), wsp(), vec(), wsp(), vec(), vec(),
        ],
        out_specs=pl.BlockSpec((1, seq, w), lambda b, j: (b, 0, j)),
        scratch_shapes=[pltpu.VMEM((nslab, seq + V7X_SUBLANES, V7X_LANES), F32)]
                       + [pltpu.VMEM((nslab, scan_rows, V7X_LANES), F32)] * 4,
        compiler_params=_cparams(("parallel", "parallel")),
        name="rg_lru",
    )(proj, proj, conv_w, conv_b, w_r, b_r, w_i, b_i, lam)


def _mprep_kernel(mx_ref, cw_ref, cb_ref, wq_ref, wk_ref, wv_ref, wif_ref, bif_ref,
                  q_ref, k_ref, v_ref, gate_ref, xs_ref):
    ts = mx_ref.shape[1]

    @pl.when(pl.program_id(1) == 0)
    def _():
        for s in range(xs_ref.shape[0]):
            xs_ref[s, pl.ds(ts, V7X_SUBLANES), :] = jnp.zeros((V7X_SUBLANES, V7X_LANES), F32)

    mxb = mx_ref[0]
    xc = []
    for s in range(xs_ref.shape[0]):
        cs = slice(s * V7X_LANES, (s + 1) * V7X_LANES)
        xs_ref[s, pl.ds(0, V7X_SUBLANES), :] = xs_ref[s, pl.ds(ts, V7X_SUBLANES), :]
        xs_ref[s, pl.ds(V7X_SUBLANES, ts), :] = mxb[:, cs].astype(F32)
        y = _causal_conv(xs_ref.at[s], ts, cw_ref[0, :, cs], cb_ref[0, :, cs])
        xc.append(_silu(y).astype(BF16))
    xc = jnp.concatenate(xc, axis=1)
    acc = jnp.zeros((ts, V7X_LANES), F32) + bif_ref[0]
    nh = MLSTM_HEADS
    for h in range(nh):
        cs = slice(h * MLSTM_DH, (h + 1) * MLSTM_DH)
        mq = _dot(xc[:, cs], wq_ref[0, h]).astype(BF16)
        mk = _dot(xc[:, cs], wk_ref[0, h]).astype(BF16)
        mv = _dot(mxb[:, cs], wv_ref[0, h]).astype(BF16)
        q_ref[0, :, cs] = mq
        k_ref[0, :, cs] = mk
        v_ref[0, :, cs] = mv
        acc = acc + _dot(mq, wif_ref[0, h]) + _dot(mk, wif_ref[0, nh + h]) + _dot(mv, wif_ref[0, 2 * nh + h])
    lane = lax.broadcasted_iota(jnp.int32, acc.shape, 1)
    log_f = jnp.minimum(acc, 0.0) - jnp.log1p(jnp.exp(-jnp.abs(acc)))
    gate_ref[0] = jnp.where(lane >= nh, log_f, acc)


def _mlstm_prep(proj, conv_w, conv_b, wq, wk, wv, wif, bif, l):
    bsz, seq, _ = proj.shape
    width = MLSTM_HEADS * MLSTM_DH
    ts = min(seq, 512)
    nh, dh = MLSTM_HEADS, MLSTM_DH
    qkv = jax.ShapeDtypeStruct((bsz, seq, width), BF16)
    wspec = lambda: pl.BlockSpec((1, nh, dh, dh), lambda b, t: (l, 0, 0, 0))
    ospec = lambda: pl.BlockSpec((1, ts, width), lambda b, t: (b, t, 0))
    return pl.pallas_call(
        _mprep_kernel,
        out_shape=(qkv, qkv, qkv, jax.ShapeDtypeStruct((bsz, seq, V7X_LANES), F32)),
        grid=(bsz, seq // ts),
        in_specs=[
            pl.BlockSpec((1, ts, width), lambda b, t: (b, t, OFF_MX // width)),
            pl.BlockSpec((1, CONV_W, width), lambda b, t: (l, 0, 0)),
            pl.BlockSpec((1, 1, width), lambda b, t: (l, 0, 0)),
            wspec(), wspec(), wspec(),
            pl.BlockSpec((1, 3 * nh, dh, V7X_LANES), lambda b, t: (l, 0, 0, 0)),
            pl.BlockSpec((1, 1, V7X_LANES), lambda b, t: (l, 0, 0)),
        ],
        out_specs=(ospec(), ospec(), ospec(),
                   pl.BlockSpec((1, ts, V7X_LANES), lambda b, t: (b, t, 0))),
        scratch_shapes=[pltpu.VMEM((width // V7X_LANES, ts + V7X_SUBLANES, V7X_LANES), F32)],
        compiler_params=_cparams(("parallel", "arbitrary")),
        name="mlstm_prep",
    )(proj, conv_w, conv_b, wq, wk, wv, wif, bif)


def _mlstm_kernel(q_ref, k_ref, v_ref, gate_ref, mo_ref, mn_ref, o_ref, c_ref, n_ref, m_ref):
    head = pl.program_id(1)

    @pl.when(pl.program_id(2) == 0)
    def _():
        c_ref[...] = jnp.zeros_like(c_ref)
        n_ref[...] = jnp.zeros_like(n_ref)
        m_ref[...] = jnp.zeros_like(m_ref)

    gates = gate_ref[0]
    ts = gates.shape[0]
    nc = ts // CHUNK
    lane = lax.broadcasted_iota(jnp.int32, gates.shape, 1)
    ic = jnp.sum(jnp.where(lane == head, gates, 0.0), axis=1, keepdims=True).reshape(nc, CHUNK, 1)
    lf = jnp.sum(jnp.where(lane == head + MLSTM_HEADS, gates, 0.0), axis=1, keepdims=True).reshape(nc, CHUNK, 1)

    ii = lax.broadcasted_iota(jnp.int32, (nc, CHUNK, CHUNK), 1)
    jj = lax.broadcasted_iota(jnp.int32, (nc, CHUNK, CHUNK), 2)
    causal = ii >= jj
    diag = ii == jj
    b_row = jnp.sum(jnp.where(ii <= jj, lf, 0.0), axis=1, keepdims=True)
    lf_row = jnp.sum(jnp.where(diag, lf, 0.0), axis=1, keepdims=True)
    ic_row = jnp.sum(jnp.where(diag, ic, 0.0), axis=1, keepdims=True)
    b_col = jnp.sum(jnp.where(causal, lf_row, 0.0), axis=2, keepdims=True)
    dmat = jnp.where(causal, b_col - b_row + ic_row, -jnp.inf)
    row_max = jnp.max(dmat, axis=2, keepdims=True)

    m_s = m_ref[...]
    m_t_list, m_prev_list = [], []
    for c in range(nc):
        m_prev_list.append(m_s)
        m_tc = jnp.maximum(b_col[c] + m_s, row_max[c])
        m_t_list.append(m_tc)
        m_s = m_tc[CHUNK - 1:]
    m_ref[...] = m_s

    q = q_ref[0]
    v = v_ref[0]
    kb = k_ref[0] * jnp.asarray(MLSTM_DH ** -0.5, BF16)
    m_t = jnp.stack(m_t_list)
    m_prev = jnp.stack(m_prev_list)
    w_inter = jnp.exp(b_col + m_prev - m_t)
    b_last = b_col[:, CHUNK - 1:, :]
    m_new = m_t[:, CHUNK - 1:, :]
    w_k = jnp.exp(b_last - b_col + ic - m_new)
    decay = jnp.exp(b_last + m_prev - m_new)
    p = jnp.exp(dmat - m_t)
    kwb = kb.reshape(nc, CHUNK, MLSTM_DH) * w_k.astype(BF16)
    ones = jnp.ones((V7X_SUBLANES, CHUNK), BF16)

    chunks = [slice(c * CHUNK, (c + 1) * CHUNK) for c in range(nc)]
    qk = [_dot_nt(q[sl], kb[sl]) for sl in chunks]
    kv = [_dot_tn(kwb[c], v[sl]) for c, sl in enumerate(chunks)]
    n_add = [_dot(ones, kwb[c]) for c in range(nc)]
    s = [qk[c] * p[c] for c in range(nc)]
    intra = [_dot(s[c].astype(BF16), v[sl]) for c, sl in enumerate(chunks)]
    c_s = c_ref[...]
    n_s = n_ref[...]
    c_states, n_states = [], []
    for c in range(nc):
        c_states.append(c_s.astype(BF16))
        n_states.append(n_s)
        c_s = decay[c] * c_s + kv[c]
        n_s = decay[c] * n_s + n_add[c][:1]
    c_ref[...] = c_s
    n_ref[...] = n_s
    inter = [_dot(q[sl], c_states[c]) for c, sl in enumerate(chunks)]
    w_inter = w_inter.reshape(ts, 1)
    num = jnp.concatenate(intra, axis=0) + w_inter * jnp.concatenate(inter, axis=0)
    s_sum = jnp.sum(jnp.stack(s), axis=2, keepdims=True).reshape(ts, 1)
    qn = [_dot_nt(q[sl], jnp.broadcast_to(n_states[c], (V7X_SUBLANES, MLSTM_DH)).astype(BF16))[:, :1]
          for c, sl in enumerate(chunks)]
    den = s_sum + w_inter * jnp.concatenate(qn, axis=0)
    h = num / jnp.maximum(jnp.abs(den), jnp.exp(-m_t.reshape(ts, 1)))
    o = _sigmoid(mo_ref[0].astype(F32)) * h
    o = o * lax.rsqrt(jnp.mean(o * o, axis=-1, keepdims=True) + EPS)
    o_ref[0] = (o * mn_ref[0]).astype(BF16)


def _mlstm(mq, mk, mv, gates, proj, m_norm, l):
    bsz, seq, width = mq.shape
    dh = MLSTM_DH
    ts = min(seq, 2048)
    hspec = lambda: pl.BlockSpec((1, ts, dh), lambda b, h, t: (b, t, h))
    return pl.pallas_call(
        _mlstm_kernel,
        out_shape=jax.ShapeDtypeStruct((bsz, seq, width), BF16),
        grid=(bsz, MLSTM_HEADS, seq // ts),
        in_specs=[
            hspec(), hspec(), hspec(),
            pl.BlockSpec((1, ts, V7X_LANES), lambda b, h, t: (b, t, 0)),
            pl.BlockSpec((1, ts, dh), lambda b, h, t: (b, t, OFF_MO // dh + h)),
            pl.BlockSpec((1, 1, dh), lambda b, h, t: (l, 0, h)),
        ],
        out_specs=hspec(),
        scratch_shapes=[pltpu.VMEM((dh, dh), F32), pltpu.VMEM((1, dh), F32), pltpu.VMEM((1, 1), F32)],
        compiler_params=_cparams(("parallel", "parallel", "arbitrary")),
        name="mlstm",
    )(mq, mk, mv, gates, proj, m_norm)


def _merge_kernel(x_ref, ret_ref, lru_ref, mls_ref, g0_ref, g1_ref, g2_ref, mod_ref,
                  wr_ref, wl_ref, wm_ref, wo_ref, o_ref):
    merged = (_sigmoid(g0_ref[0].astype(F32)) * _dot(ret_ref[0], wr_ref[0])
              + _sigmoid(g1_ref[0].astype(F32)) * _dot(lru_ref[0], wl_ref[0])
              + _sigmoid(g2_ref[0].astype(F32)) * _dot(mls_ref[0], wm_ref[0]))
    y = _dot(merged.astype(BF16), wo_ref[0])
    o_ref[0] = x_ref[0] + mod_ref[0, 0, 2:3, :] * y


def _merge(x, ret, lru, mls, proj, mod, w_br_ret, w_br_lru, w_br_mlstm, w_out, l):
    bsz, seq, d = x.shape
    tm = min(seq, 512)
    gb = OFF_GATE // d
    tok = lambda: pl.BlockSpec((1, tm, d), lambda b, i: (b, i, 0))
    wsp = lambda: pl.BlockSpec((1, d, d), lambda b, i: (l, 0, 0))
    return pl.pallas_call(
        _merge_kernel,
        out_shape=jax.ShapeDtypeStruct((bsz, seq, d), F32),
        grid=(bsz, seq // tm),
        in_specs=[
            tok(), tok(), tok(), tok(),
            pl.BlockSpec((1, tm, d), lambda b, i: (b, i, gb)),
            pl.BlockSpec((1, tm, d), lambda b, i: (b, i, gb + 1)),
            pl.BlockSpec((1, tm, d), lambda b, i: (b, i, gb + 2)),
            pl.BlockSpec((1, 1, 6, d), lambda b, i: (l, b, 0, 0)),
            wsp(), wsp(), wsp(), wsp(),
        ],
        out_specs=tok(),
        compiler_params=_cparams(("parallel", "parallel")),
        name="merge_out",
    )(x, ret, lru, mls, proj, proj, proj, mod, w_br_ret, w_br_lru, w_br_mlstm, w_out)


def _ffn_kernel(x_ref, mod_ref, gain_ref, w1_ref, w2_ref, *rest, last):
    x = x_ref[0]
    h = _modulated_norm(x, gain_ref[0], mod_ref[0, 0, 4:5, :], mod_ref[0, 0, 3:4, :])
    a = jnp.square(jnp.maximum(_dot(h.astype(BF16), w1_ref[0]), 0.0))
    y = x + mod_ref[0, 0, 5:6, :] * _dot(a.astype(BF16), w2_ref[0])
    if last:
        fgain_ref, o_ref = rest
        o_ref[0] = y * lax.rsqrt(jnp.mean(y * y, axis=-1, keepdims=True) + EPS) * fgain_ref[...]
    else:
        nmod_ref, ngain_ref, o_ref, hn_ref = rest
        o_ref[0] = y
        hn = _modulated_norm(y, ngain_ref[0], nmod_ref[0, 0, 1:2, :], nmod_ref[0, 0, 0:1, :])
        hn_ref[0] = hn.astype(BF16)


def _ffn(x, mod, gain, w1, w2, mix_gain, final_gain, l, last):
    bsz, seq, d = x.shape
    dff = w1.shape[-1]
    tm = min(seq, 512)
    tok = lambda: pl.BlockSpec((1, tm, d), lambda b, i: (b, i, 0))
    in_specs = [
        tok(),
        pl.BlockSpec((1, 1, 6, d), lambda b, i: (l, b, 0, 0)),
        pl.BlockSpec((1, 1, d), lambda b, i: (l, 0, 0)),
        pl.BlockSpec((1, d, dff), lambda b, i: (l, 0, 0), pipeline_mode=pl.Buffered(1)),
        pl.BlockSpec((1, dff, d), lambda b, i: (l, 0, 0), pipeline_mode=pl.Buffered(1)),
    ]
    if last:
        in_specs.append(pl.BlockSpec((1, d), lambda b, i: (0, 0)))
        extra = (final_gain,)
        out_shape = jax.ShapeDtypeStruct((bsz, seq, d), F32)
        out_specs = tok()
    else:
        in_specs += [pl.BlockSpec((1, 1, 6, d), lambda b, i: (l + 1, b, 0, 0)),
                     pl.BlockSpec((1, 1, d), lambda b, i: (l + 1, 0, 0))]
        extra = (mod, mix_gain)
        out_shape = (jax.ShapeDtypeStruct((bsz, seq, d), F32), jax.ShapeDtypeStruct((bsz, seq, d), BF16))
        out_specs = (tok(), tok())
    return pl.pallas_call(
        functools.partial(_ffn_kernel, last=last),
        out_shape=out_shape,
        grid=(bsz, seq // tm),
        in_specs=in_specs,
        out_specs=out_specs,
        compiler_params=_cparams(("parallel", "parallel")),
        name="ffn",
    )(x, mod, gain, w1, w2, *extra)


def _block_diag_tiles(w, tile):
    depth, nb, bs, _ = w.shape
    rows = nb * bs
    sel = jnp.tile(jnp.eye(bs, dtype=w.dtype), (1, tile // bs))
    dense = jnp.einsum('lre,ec->lrc', w.reshape(depth, rows, bs), sel, precision=lax.Precision.HIGHEST)
    r = (np.arange(rows) % tile) // bs
    c = np.arange(tile) // bs
    dense = jnp.where(jnp.asarray(r[:, None] == c[None, :]), dense, 0.0)
    return dense.reshape(depth, rows // tile, tile, tile)


def kernel(x, c, positions, w_ada, b_ada, norm_mix, norm_mlp, w_in, lru_conv_w, lru_conv_b, lru_w_r, lru_b_r, lru_w_i, lru_b_i, lru_lambda, m_conv_w, m_conv_b, m_w_q, m_w_k, m_w_v, m_w_if, m_b_if, m_norm, w_br_ret, w_br_lru, w_br_mlstm, w_out, w_ff1, w_ff2, final_norm):
    depth = w_in.shape[0]
    bsz, seq, d = x.shape
    nh, dh = MLSTM_HEADS, MLSTM_DH

    mod = _ada(c, w_ada, b_ada).reshape(depth, bsz, 6, d)
    cc, ss = _rope_tables(positions)

    vec = lambda a: a.reshape(depth, 1, a.shape[-1])
    w_in_b = w_in.astype(BF16)
    w_r_t = _block_diag_tiles(lru_w_r, V7X_LANES).astype(BF16)
    w_i_t = _block_diag_tiles(lru_w_i, V7X_LANES).astype(BF16)
    wq_t = _block_diag_tiles(m_w_q, dh).astype(BF16)
    wk_t = _block_diag_tiles(m_w_k, dh).astype(BF16)
    wv_t = _block_diag_tiles(m_w_v, dh).astype(BF16)
    wif_t = jnp.pad(m_w_if, ((0, 0), (0, 0), (0, V7X_LANES - 2 * nh))).reshape(depth, 3 * nh, dh, V7X_LANES).astype(BF16)
    bif_t = jnp.pad(m_b_if, ((0, 0), (0, V7X_LANES - 2 * nh))).reshape(depth, 1, V7X_LANES)
    w_br_ret_b, w_br_lru_b, w_br_mls_b = w_br_ret.astype(BF16), w_br_lru.astype(BF16), w_br_mlstm.astype(BF16)
    w_out_b, w_ff1_b, w_ff2_b = w_out.astype(BF16), w_ff1.astype(BF16), w_ff2.astype(BF16)
    g_mix, g_mlp = vec(norm_mix), vec(norm_mlp)
    l_cb, l_br, l_bi, l_lam = vec(lru_conv_b), vec(lru_b_r), vec(lru_b_i), vec(lru_lambda)
    m_cb, m_nrm = vec(m_conv_b), vec(m_norm)
    f_gain = final_norm.reshape(1, d)

    h = _prenorm(x, mod, g_mix, 0)
    for l in range(depth):
        proj = _inproj(h, w_in_b, l)
        ret = _retention(proj, cc, ss)
        lru = _lru(proj, lru_conv_w, l_cb, w_r_t, l_br, w_i_t, l_bi, l_lam, l)
        mq, mk, mv, gates = _mlstm_prep(proj, m_conv_w, m_cb, wq_t, wk_t, wv_t, wif_t, bif_t, l)
        mls = _mlstm(mq, mk, mv, gates, proj, m_nrm, l)
        x = _merge(x, ret, lru, mls, proj, mod, w_br_ret_b, w_br_lru_b, w_br_mls_b, w_out_b, l)
        if l == depth - 1:
            return _ffn(x, mod, g_mlp, w_ff1_b, w_ff2_b, g_mix, f_gain, l, last=True)
        x, h = _ffn(x, mod, g_mlp, w_ff1_b, w_ff2_b, g_mix, f_gain, l, last=False)
```

```python
import functools

import jax
import jax.numpy as jnp
import numpy as np
from jax import lax
from jax.experimental import pallas as pl
from jax.experimental.pallas import tpu as pltpu

F32 = jnp.float32
BF16 = jnp.bfloat16

D_MODEL = 1024
RET_HEADS = 4
RET_DK = 128
RET_DV = 256
ROPE_BASE = 10000.0
LRU_C = 8.0
CONV_W = 4
MLSTM_HEADS = 4
MLSTM_DH = 256
CHUNK = 128
EPS = 1e-6

OFF_RQ, OFF_RK, OFF_RV, OFF_RG = 0, 512, 1024, 2048
OFF_LX, OFF_LY, OFF_MX, OFF_MO, OFF_GATE = 3072, 4096, 5120, 6144, 7168

V7X_LANES = 128
V7X_SUBLANES = 8
V7X_VMEM_LIMIT_BYTES = 56 * 1024 * 1024


def _cparams(sem):
    return pltpu.CompilerParams(dimension_semantics=sem, vmem_limit_bytes=V7X_VMEM_LIMIT_BYTES)


def _dot(a, b):
    return jnp.dot(a, b, preferred_element_type=F32)


def _dot_nt(a, b):
    return lax.dot_general(a, b, (((1,), (1,)), ((), ())), preferred_element_type=F32)


def _dot_tn(a, b):
    return lax.dot_general(a, b, (((0,), (0,)), ((), ())), preferred_element_type=F32)


def _sigmoid(x):
    return 0.5 * jnp.tanh(0.5 * x) + 0.5


def _silu(x):
    return x * _sigmoid(x)


def _ada_kernel(c_ref, w_ref, b_ref, o_ref):
    cond = _silu(c_ref[...])
    o_ref[0] = _dot(cond.astype(BF16), w_ref[0].astype(BF16)) + b_ref[0]


def _ada(c, w_ada, b_ada):
    depth, d, n = w_ada.shape
    bsz = c.shape[0]
    tn = 1024
    return pl.pallas_call(
        _ada_kernel,
        out_shape=jax.ShapeDtypeStruct((depth, bsz, n), F32),
        grid=(depth, n // tn),
        in_specs=[
            pl.BlockSpec((bsz, d), lambda l, j: (0, 0)),
            pl.BlockSpec((1, d, tn), lambda l, j: (l, 0, j)),
            pl.BlockSpec((1, 1, tn), lambda l, j: (l, 0, j)),
        ],
        out_specs=pl.BlockSpec((1, bsz, tn), lambda l, j: (l, 0, j)),
        compiler_params=_cparams(("parallel", "parallel")),
        name="ada_mod",
    )(c, w_ada, b_ada.reshape(depth, 1, n))


def _rope_kernel(pos_ref, invf_ref, cc_ref, ss_ref):
    ang = pos_ref[0] * invf_ref[...]
    lane = lax.broadcasted_iota(jnp.int32, ang.shape, 1)
    sn = jnp.sin(ang)
    cc_ref[0] = jnp.cos(ang)
    ss_ref[0] = jnp.where(lane < RET_DK // 2, -sn, sn)


def _rope_tables(positions):
    bsz, seq = positions.shape
    half = RET_DK // 2
    inv_freq = ROPE_BASE ** (-jnp.arange(half, dtype=F32) / half)
    invf = jnp.concatenate([inv_freq, inv_freq]).reshape(1, RET_DK)
    pos = positions.astype(F32).reshape(bsz, seq, 1)
    ts = min(seq, 512)
    out = jax.ShapeDtypeStruct((bsz, seq, RET_DK), F32)
    return pl.pallas_call(
        _rope_kernel,
        out_shape=(out, out),
        grid=(bsz, seq // ts),
        in_specs=[
            pl.BlockSpec((1, ts, 1), lambda b, t: (b, t, 0)),
            pl.BlockSpec((1, RET_DK), lambda b, t: (0, 0)),
        ],
        out_specs=(pl.BlockSpec((1, ts, RET_DK), lambda b, t: (b, t, 0)),
                   pl.BlockSpec((1, ts, RET_DK), lambda b, t: (b, t, 0))),
        compiler_params=_cparams(("parallel", "parallel")),
        name="rope_tables",
    )(pos, invf)


def _modulated_norm(x, gain, scale, shift):
    xn = x * lax.rsqrt(jnp.mean(x * x, axis=-1, keepdims=True) + EPS)
    return xn * gain * (1.0 + scale) + shift


def _prenorm_kernel(x_ref, mod_ref, gain_ref, o_ref):
    h = _modulated_norm(x_ref[0], gain_ref[0], mod_ref[0, 0, 1:2, :], mod_ref[0, 0, 0:1, :])
    o_ref[0] = h.astype(BF16)


def _prenorm(x, mod, gain, l):
    bsz, seq, d = x.shape
    tm = min(seq, 1024)
    tok = lambda: pl.BlockSpec((1, tm, d), lambda b, i: (b, i, 0))
    return pl.pallas_call(
        _prenorm_kernel,
        out_shape=jax.ShapeDtypeStruct((bsz, seq, d), BF16),
        grid=(bsz, seq // tm),
        in_specs=[
            tok(),
            pl.BlockSpec((1, 1, 6, d), lambda b, i: (l, b, 0, 0)),
            pl.BlockSpec((1, 1, d), lambda b, i: (l, 0, 0)),
        ],
        out_specs=tok(),
        compiler_params=_cparams(("parallel", "parallel")),
        name="prenorm",
    )(x, mod, gain)


INPROJ_N_SPLIT = 2


def _inproj_kernel(h_ref, w_ref, o_ref, wb_ref):
    @pl.when((pl.program_id(1) == 0) & (pl.program_id(2) == 0))
    def _():
        wb_ref[...] = w_ref[0].astype(BF16)

    part = wb_ref.shape[1] // INPROJ_N_SPLIT
    for n in range(INPROJ_N_SPLIT):
        cs = slice(n * part, (n + 1) * part)
        o_ref[0, :, cs] = _dot(h_ref[0], wb_ref[:, cs]).astype(BF16)


def _inproj(h, w_in, l):
    bsz, seq, d = h.shape
    n = w_in.shape[-1]
    tm = min(seq, 2048)
    tn = 2048
    return pl.pallas_call(
        _inproj_kernel,
        out_shape=jax.ShapeDtypeStruct((bsz, seq, n), BF16),
        grid=(n // tn, bsz, seq // tm),
        in_specs=[
            pl.BlockSpec((1, tm, d), lambda j, b, i: (b, i, 0)),
            pl.BlockSpec((1, d, tn), lambda j, b, i: (l, 0, j)),
        ],
        out_specs=pl.BlockSpec((1, tm, tn), lambda j, b, i: (b, i, j)),
        scratch_shapes=[pltpu.VMEM((d, tn), BF16)],
        compiler_params=_cparams(("arbitrary", "arbitrary", "arbitrary")),
        name="in_proj",
    )(h, w_in)


def _ret_kernel(q_ref, k_ref, v_ref, g_ref, cc_ref, ss_ref, o_ref, state_ref):
    head = pl.program_id(1)

    @pl.when(pl.program_id(2) == 0)
    def _():
        state_ref[...] = jnp.zeros_like(state_ref)

    hv = jnp.zeros((1, 1), F32) + head.astype(F32)
    log_gamma = jnp.log1p(-jnp.exp2(-5.0 - hv))
    ii = lax.broadcasted_iota(jnp.int32, (CHUNK, CHUNK), 0)
    jj = lax.broadcasted_iota(jnp.int32, (CHUNK, CHUNK), 1)
    causal = ii >= jj
    diff = jnp.where(causal, (ii - jj).astype(F32), 0.0)
    decay_intra = jnp.where(causal, jnp.exp(log_gamma * diff), 0.0)
    pos = lax.broadcasted_iota(jnp.int32, (CHUNK, 1), 0).astype(F32)
    decay_q = jnp.exp(log_gamma * (pos + 1.0))
    decay_k = jnp.exp(log_gamma * (CHUNK - 1.0 - pos))
    decay_chunk = jnp.exp(log_gamma * CHUNK)

    cc = cc_ref[0]
    ss = ss_ref[0]
    q = q_ref[0].astype(F32)
    k = k_ref[0].astype(F32)
    q = q * cc + pltpu.roll(q, RET_DK // 2, 1) * ss
    k = (k * cc + pltpu.roll(k, RET_DK // 2, 1) * ss) * RET_DK ** -0.5

    ts = q.shape[0]
    nc = ts // CHUNK
    qb = q.astype(BF16)
    kb = k.astype(BF16)
    qd = (q.reshape(nc, CHUNK, RET_DK) * decay_q).astype(BF16)
    kd = (k.reshape(nc, CHUNK, RET_DK) * decay_k).astype(BF16)
    v = v_ref[0]
    chunks = [slice(c * CHUNK, (c + 1) * CHUNK) for c in range(nc)]
    scores = [_dot_nt(qb[sl], kb[sl]) for sl in chunks]
    kv = [_dot_tn(kd[c], v[sl]) for c, sl in enumerate(chunks)]
    scores = [(s * decay_intra).astype(BF16) for s in scores]
    intra = [_dot(scores[c], v[sl]) for c, sl in enumerate(chunks)]
    state = state_ref[...]
    states = []
    for c in range(nc):
        states.append(state.astype(BF16))
        state = decay_chunk * state + kv[c]
    state_ref[...] = state
    inter = [_dot(qd[c], states[c]) for c in range(nc)]
    out = jnp.concatenate(intra, axis=0) + jnp.concatenate(inter, axis=0)
    out = out * lax.rsqrt(jnp.mean(out * out, axis=-1, keepdims=True) + EPS)
    o_ref[0] = (out * _silu(g_ref[0].astype(F32))).astype(BF16)


def _retention(proj, cc, ss):
    bsz, seq, _ = proj.shape
    ts = min(seq, 2048)
    qb, kb = OFF_RQ // RET_DK, OFF_RK // RET_DK
    vb, gb = OFF_RV // RET_DV, OFF_RG // RET_DV
    return pl.pallas_call(
        _ret_kernel,
        out_shape=jax.ShapeDtypeStruct((bsz, seq, RET_HEADS * RET_DV), BF16),
        grid=(bsz, RET_HEADS, seq // ts),
        in_specs=[
            pl.BlockSpec((1, ts, RET_DK), lambda b, h, t: (b, t, qb + h)),
            pl.BlockSpec((1, ts, RET_DK), lambda b, h, t: (b, t, kb + h)),
            pl.BlockSpec((1, ts, RET_DV), lambda b, h, t: (b, t, vb + h)),
            pl.BlockSpec((1, ts, RET_DV), lambda b, h, t: (b, t, gb + h)),
            pl.BlockSpec((1, ts, RET_DK), lambda b, h, t: (b, t, 0)),
            pl.BlockSpec((1, ts, RET_DK), lambda b, h, t: (b, t, 0)),
        ],
        out_specs=pl.BlockSpec((1, ts, RET_DV), lambda b, h, t: (b, t, h)),
        scratch_shapes=[pltpu.VMEM((RET_DK, RET_DV), F32)],
        compiler_params=_cparams(("parallel", "parallel", "arbitrary")),
        name="retention",
    )(proj, proj, proj, proj, cc, ss)


def _causal_conv(xs_ref, rows, w, b):
    y = b
    for k in range(CONV_W):
        s = CONV_W - 1 - k
        y = y + w[k:k + 1, :] * xs_ref[pl.ds(V7X_SUBLANES - s, rows), :]
    return y


LRU_COLS = 256
LRU_SEGS = V7X_SUBLANES


def _lru_pitch(seg_len):
    tiles = seg_len // V7X_SUBLANES
    return (tiles + 1 - tiles % 2) * V7X_SUBLANES


def _lru_kernel(lx_ref, ly_ref, cw_ref, cb_ref, wr_ref, br_ref, wi_ref, bi_ref, lam_ref, o_ref,
                xs_ref, a_ref, u_ref, h_ref, p_ref):
    seq = lx_ref.shape[1]
    nslab = lx_ref.shape[2] // V7X_LANES
    seg_len = seq // LRU_SEGS
    pitch = _lru_pitch(seg_len)
    zeros8 = jnp.zeros((V7X_SUBLANES, V7X_LANES), F32)

    for s in range(nslab):
        cs = slice(s * V7X_LANES, (s + 1) * V7X_LANES)
        xs_ref[s, pl.ds(0, V7X_SUBLANES), :] = zeros8
        xs_ref[s, pl.ds(V7X_SUBLANES, seq), :] = lx_ref[0, :, cs].astype(F32)
        xl = _causal_conv(xs_ref.at[s], seq, cw_ref[0, :, cs], cb_ref[0, :, cs])
        xb = xl.astype(BF16)
        tr = jnp.tanh(0.5 * (_dot(xb, wr_ref[0, s]) + br_ref[0, :, cs]))
        ti = jnp.tanh(0.5 * (_dot(xb, wi_ref[0, s]) + bi_ref[0, :, cs]))
        lam = lam_ref[0, :, cs]
        softplus_neg_lam = jnp.maximum(-lam, 0.0) + jnp.log1p(jnp.exp(-jnp.abs(lam)))
        log_a = (tr + 1.0) * (-0.5 * LRU_C * softplus_neg_lam)
        a = jnp.exp(log_a)
        y = -jnp.tanh(log_a) * (a * a + 1.0)
        u = jnp.where(y > 0.0, y * lax.rsqrt(y), 0.0) * ((0.5 * ti + 0.5) * xl)
        for g in range(LRU_SEGS):
            a_ref[s, pl.ds(g * pitch, seg_len), :] = a[g * seg_len:(g + 1) * seg_len]
            u_ref[s, pl.ds(g * pitch, seg_len), :] = u[g * seg_len:(g + 1) * seg_len]

    def body(t, carry):
        hs, ps = carry
        new_h, new_p = [], []
        for s in range(nslab):
            rows = pl.ds(t, LRU_SEGS, stride=pitch)
            av = a_ref[s, rows, :]
            h = av * hs[s] + u_ref[s, rows, :]
            p = av * ps[s]
            h_ref[s, rows, :] = h
            p_ref[s, rows, :] = p
            new_h.append(h)
            new_p.append(p)
        return tuple(new_h), tuple(new_p)

    init = (tuple(zeros8 for _ in range(nslab)), tuple(zeros8 + 1.0 for _ in range(nslab)))
    h_end, p_end = lax.fori_loop(0, seg_len, body, init, unroll=8)

    for s in range(nslab):
        cs = slice(s * V7X_LANES, (s + 1) * V7X_LANES)
        carry = jnp.zeros((1, V7X_LANES), F32)
        for g in range(LRU_SEGS):
            rows = pl.ds(g * pitch, seg_len)
            h = h_ref[s, rows, :] + p_ref[s, rows, :] * carry
            gate = jax.nn.gelu(ly_ref[0, g * seg_len:(g + 1) * seg_len, cs].astype(F32))
            o_ref[0, g * seg_len:(g + 1) * seg_len, cs] = (h * gate).astype(BF16)
            carry = h_end[s][g:g + 1] + p_end[s][g:g + 1] * carry


def _lru(proj, conv_w, conv_b, w_r, b_r, w_i, b_i, lam, l):
    bsz, seq, _ = proj.shape
    w = LRU_COLS
    nslab = w // V7X_LANES
    scan_rows = LRU_SEGS * _lru_pitch(seq // LRU_SEGS)
    vec = lambda: pl.BlockSpec((1, 1, w), lambda b, j: (l, 0, j))
    wsp = lambda: pl.BlockSpec((1, nslab, V7X_LANES, V7X_LANES), lambda b, j: (l, j, 0, 0))
    return pl.pallas_call(
        _lru_kernel,
        out_shape=jax.ShapeDtypeStruct((bsz, seq, D_MODEL), BF16),
        grid=(bsz, D_MODEL // w),
        in_specs=[
            pl.BlockSpec((1, seq, w), lambda b, j: (b, 0, OFF_LX // w + j)),
            pl.BlockSpec((1, seq, w), lambda b, j: (b, 0, OFF_LY // w + j)),
            pl.BlockSpec((1, CONV_W, w), lambda b, j: (l, 0, j)),
            vec(), wsp(), vec(), wsp(), vec(), vec(),
        ],
        out_specs=pl.BlockSpec((1, seq, w), lambda b, j: (b, 0, j)),
        scratch_shapes=[pltpu.VMEM((nslab, seq + V7X_SUBLANES, V7X_LANES), F32)]
                       + [pltpu.VMEM((nslab, scan_rows, V7X_LANES), F32)] * 4,
        compiler_params=_cparams(("parallel", "parallel")),
        name="rg_lru",
    )(proj, proj, conv_w, conv_b, w_r, b_r, w_i, b_i, lam)


def _mprep_kernel(mx_ref, cw_ref, cb_ref, wq_ref, wk_ref, wv_ref, wif_ref, bif_ref, *rest):
    wf_refs, (q_ref, k_ref, v_ref, gate_ref), wb_refs, xs_ref = rest[:4], rest[4:8], rest[8:12], rest[12]
    for wf_ref, wb_ref in zip(wf_refs, wb_refs):
        wb_ref[...] = wf_ref[0].astype(BF16)
    ts = mx_ref.shape[1]

    @pl.when(pl.program_id(1) == 0)
    def _():
        for s in range(xs_ref.shape[0]):
            xs_ref[s, pl.ds(ts, V7X_SUBLANES), :] = jnp.zeros((V7X_SUBLANES, V7X_LANES), F32)

    mxb = mx_ref[0]
    xc = []
    for s in range(xs_ref.shape[0]):
        cs = slice(s * V7X_LANES, (s + 1) * V7X_LANES)
        xs_ref[s, pl.ds(0, V7X_SUBLANES), :] = xs_ref[s, pl.ds(ts, V7X_SUBLANES), :]
        xs_ref[s, pl.ds(V7X_SUBLANES, ts), :] = mxb[:, cs].astype(F32)
        y = _causal_conv(xs_ref.at[s], ts, cw_ref[0, :, cs], cb_ref[0, :, cs])
        xc.append(_silu(y).astype(BF16))
    xc = jnp.concatenate(xc, axis=1)
    acc = jnp.zeros((ts, V7X_LANES), F32) + bif_ref[0]
    nh = MLSTM_HEADS
    for h in range(nh):
        cs = slice(h * MLSTM_DH, (h + 1) * MLSTM_DH)
        mq = _dot(xc[:, cs], wq_ref[0, h]).astype(BF16)
        mk = _dot(xc[:, cs], wk_ref[0, h]).astype(BF16)
        mv = _dot(mxb[:, cs], wv_ref[0, h]).astype(BF16)
        q_ref[0, :, cs] = mq
        k_ref[0, :, cs] = mk
        v_ref[0, :, cs] = mv
        acc = acc + _dot(mq, wif_ref[0, h]) + _dot(mk, wif_ref[0, nh + h]) + _dot(mv, wif_ref[0, 2 * nh + h])
    lane = lax.broadcasted_iota(jnp.int32, acc.shape, 1)
    log_f = jnp.minimum(acc, 0.0) - jnp.log1p(jnp.exp(-jnp.abs(acc)))
    gate_ref[0] = jnp.where(lane >= nh, log_f, acc)


def _mlstm_prep(proj, conv_w, conv_b, wq, wk, wv, wif, bif, merge_ws, l):
    bsz, seq, _ = proj.shape
    width = MLSTM_HEADS * MLSTM_DH
    ts = min(seq, 512)
    nt = seq // ts
    nh, dh = MLSTM_HEADS, MLSTM_DH
    d = merge_ws[0].shape[-1]
    rows = merge_ws[0].shape[1] // (bsz * nt)
    qkv = jax.ShapeDtypeStruct((bsz, seq, width), BF16)
    wb_shape = jax.ShapeDtypeStruct(merge_ws[0].shape[1:], BF16)
    wspec = lambda: pl.BlockSpec((1, nh, dh, dh), lambda b, t: (l, 0, 0, 0))
    ospec = lambda: pl.BlockSpec((1, ts, width), lambda b, t: (b, t, 0))
    return pl.pallas_call(
        _mprep_kernel,
        out_shape=(qkv, qkv, qkv, jax.ShapeDtypeStruct((bsz, seq, V7X_LANES), F32)) + (wb_shape,) * 4,
        grid=(bsz, nt),
        in_specs=[
            pl.BlockSpec((1, ts, width), lambda b, t: (b, t, OFF_MX // width)),
            pl.BlockSpec((1, CONV_W, width), lambda b, t: (l, 0, 0)),
            pl.BlockSpec((1, 1, width), lambda b, t: (l, 0, 0)),
            wspec(), wspec(), wspec(),
            pl.BlockSpec((1, 3 * nh, dh, V7X_LANES), lambda b, t: (l, 0, 0, 0)),
            pl.BlockSpec((1, 1, V7X_LANES), lambda b, t: (l, 0, 0)),
        ] + [pl.BlockSpec((1, rows, d), lambda b, t: (l, b * nt + t, 0))] * 4,
        out_specs=(ospec(), ospec(), ospec(),
                   pl.BlockSpec((1, ts, V7X_LANES), lambda b, t: (b, t, 0)))
                  + (pl.BlockSpec((rows, d), lambda b, t: (b * nt + t, 0)),) * 4,
        scratch_shapes=[pltpu.VMEM((width // V7X_LANES, ts + V7X_SUBLANES, V7X_LANES), F32)],
        compiler_params=_cparams(("parallel", "arbitrary")),
        name="mlstm_prep",
    )(proj, conv_w, conv_b, wq, wk, wv, wif, bif, *merge_ws)


def _mlstm_kernel(q_ref, k_ref, v_ref, gate_ref, mo_ref, mn_ref, o_ref, c_ref, n_ref, m_ref):
    head = pl.program_id(1)

    @pl.when(pl.program_id(2) == 0)
    def _():
        c_ref[...] = jnp.zeros_like(c_ref)
        n_ref[...] = jnp.zeros_like(n_ref)
        m_ref[...] = jnp.zeros_like(m_ref)

    gates = gate_ref[0]
    ts = gates.shape[0]
    nc = ts // CHUNK
    lane = lax.broadcasted_iota(jnp.int32, gates.shape, 1)
    ic = jnp.sum(jnp.where(lane == head, gates, 0.0), axis=1, keepdims=True).reshape(nc, CHUNK, 1)
    lf = jnp.sum(jnp.where(lane == head + MLSTM_HEADS, gates, 0.0), axis=1, keepdims=True).reshape(nc, CHUNK, 1)

    ii = lax.broadcasted_iota(jnp.int32, (nc, CHUNK, CHUNK), 1)
    jj = lax.broadcasted_iota(jnp.int32, (nc, CHUNK, CHUNK), 2)
    causal = ii >= jj
    diag = ii == jj
    b_row = jnp.sum(jnp.where(ii <= jj, lf, 0.0), axis=1, keepdims=True)
    lf_row = jnp.sum(jnp.where(diag, lf, 0.0), axis=1, keepdims=True)
    ic_row = jnp.sum(jnp.where(diag, ic, 0.0), axis=1, keepdims=True)
    b_col = jnp.sum(jnp.where(causal, lf_row, 0.0), axis=2, keepdims=True)
    dmat = jnp.where(causal, b_col - b_row + ic_row, -jnp.inf)
    row_max = jnp.max(dmat, axis=2, keepdims=True)

    m_s = m_ref[...]
    m_t_list, m_prev_list = [], []
    for c in range(nc):
        m_prev_list.append(m_s)
        m_tc = jnp.maximum(b_col[c] + m_s, row_max[c])
        m_t_list.append(m_tc)
        m_s = m_tc[CHUNK - 1:]
    m_ref[...] = m_s

    q = q_ref[0]
    v = v_ref[0]
    kb = k_ref[0] * jnp.asarray(MLSTM_DH ** -0.5, BF16)
    m_t = jnp.stack(m_t_list)
    m_prev = jnp.stack(m_prev_list)
    w_inter = jnp.exp(b_col + m_prev - m_t)
    b_last = b_col[:, CHUNK - 1:, :]
    m_new = m_t[:, CHUNK - 1:, :]
    w_k = jnp.exp(b_last - b_col + ic - m_new)
    decay = jnp.exp(b_last + m_prev - m_new)
    p = jnp.exp(dmat - m_t)
    kwb = kb.reshape(nc, CHUNK, MLSTM_DH) * w_k.astype(BF16)
    ones = jnp.ones((V7X_SUBLANES, CHUNK), BF16)

    chunks = [slice(c * CHUNK, (c + 1) * CHUNK) for c in range(nc)]
    qk = [_dot_nt(q[sl], kb[sl]) for sl in chunks]
    kv = [_dot_tn(kwb[c], v[sl]) for c, sl in enumerate(chunks)]
    n_add = [_dot(ones, kwb[c]) for c in range(nc)]
    s = [qk[c] * p[c] for c in range(nc)]
    intra = [_dot(s[c].astype(BF16), v[sl]) for c, sl in enumerate(chunks)]
    c_s = c_ref[...]
    n_s = n_ref[...]
    c_states, n_states = [], []
    for c in range(nc):
        c_states.append(c_s.astype(BF16))
        n_states.append(n_s)
        c_s = decay[c] * c_s + kv[c]
        n_s = decay[c] * n_s + n_add[c][:1]
    c_ref[...] = c_s
    n_ref[...] = n_s
    inter = [_dot(q[sl], c_states[c]) for c, sl in enumerate(chunks)]
    w_inter = w_inter.reshape(ts, 1)
    num = jnp.concatenate(intra, axis=0) + w_inter * jnp.concatenate(inter, axis=0)
    s_sum = jnp.sum(jnp.stack(s), axis=2, keepdims=True).reshape(ts, 1)
    qn = [_dot_nt(q[sl], jnp.broadcast_to(n_states[c], (V7X_SUBLANES, MLSTM_DH)).astype(BF16))[:, :1]
          for c, sl in enumerate(chunks)]
    den = s_sum + w_inter * jnp.concatenate(qn, axis=0)
    h = num / jnp.maximum(jnp.abs(den), jnp.exp(-m_t.reshape(ts, 1)))
    o = _sigmoid(mo_ref[0].astype(F32)) * h
    o = o * lax.rsqrt(jnp.mean(o * o, axis=-1, keepdims=True) + EPS)
    o_ref[0] = (o * mn_ref[0]).astype(BF16)


def _mlstm(mq, mk, mv, gates, proj, m_norm, l):
    bsz, seq, width = mq.shape
    dh = MLSTM_DH
    ts = min(seq, 2048)
    hspec = lambda: pl.BlockSpec((1, ts, dh), lambda b, h, t: (b, t, h))
    return pl.pallas_call(
        _mlstm_kernel,
        out_shape=jax.ShapeDtypeStruct((bsz, seq, width), BF16),
        grid=(bsz, MLSTM_HEADS, seq // ts),
        in_specs=[
            hspec(), hspec(), hspec(),
            pl.BlockSpec((1, ts, V7X_LANES), lambda b, h, t: (b, t, 0)),
            pl.BlockSpec((1, ts, dh), lambda b, h, t: (b, t, OFF_MO // dh + h)),
            pl.BlockSpec((1, 1, dh), lambda b, h, t: (l, 0, h)),
        ],
        out_specs=hspec(),
        scratch_shapes=[pltpu.VMEM((dh, dh), F32), pltpu.VMEM((1, dh), F32), pltpu.VMEM((1, 1), F32)],
        compiler_params=_cparams(("parallel", "parallel", "arbitrary")),
        name="mlstm",
    )(mq, mk, mv, gates, proj, m_norm)


def _merge_kernel(x_ref, ret_ref, lru_ref, mls_ref, g0_ref, g1_ref, g2_ref, mod_ref,
                  wr_ref, wl_ref, wm_ref, wo_ref, w1f_ref, w2f_ref, o_ref, w1b_ref, w2b_ref):
    w1b_ref[...] = w1f_ref[0].astype(BF16)
    w2b_ref[...] = w2f_ref[0].astype(BF16)
    merged = (_sigmoid(g0_ref[0].astype(F32)) * _dot(ret_ref[0], wr_ref[...])
              + _sigmoid(g1_ref[0].astype(F32)) * _dot(lru_ref[0], wl_ref[...])
              + _sigmoid(g2_ref[0].astype(F32)) * _dot(mls_ref[0], wm_ref[...]))
    y = _dot(merged.astype(BF16), wo_ref[...])
    o_ref[0] = x_ref[0] + mod_ref[0, 0, 2:3, :] * y


def _merge(x, ret, lru, mls, proj, mod, w_br_ret, w_br_lru, w_br_mlstm, w_out, w_ff1, w_ff2, l):
    bsz, seq, d = x.shape
    dff = w_ff1.shape[-1]
    tm = min(seq, 512)
    nt = seq // tm
    steps = bsz * nt
    r1, r2 = d // steps, dff // steps
    gb = OFF_GATE // d
    tok = lambda: pl.BlockSpec((1, tm, d), lambda b, i: (b, i, 0))
    wsp = lambda: pl.BlockSpec((d, d), lambda b, i: (0, 0))
    return pl.pallas_call(
        _merge_kernel,
        out_shape=(jax.ShapeDtypeStruct((bsz, seq, d), F32),
                   jax.ShapeDtypeStruct((d, dff), BF16), jax.ShapeDtypeStruct((dff, d), BF16)),
        grid=(bsz, nt),
        in_specs=[
            tok(), tok(), tok(), tok(),
            pl.BlockSpec((1, tm, d), lambda b, i: (b, i, gb)),
            pl.BlockSpec((1, tm, d), lambda b, i: (b, i, gb + 1)),
            pl.BlockSpec((1, tm, d), lambda b, i: (b, i, gb + 2)),
            pl.BlockSpec((1, 1, 6, d), lambda b, i: (l, b, 0, 0)),
            wsp(), wsp(), wsp(), wsp(),
            pl.BlockSpec((1, r1, dff), lambda b, i: (l, b * nt + i, 0)),
            pl.BlockSpec((1, r2, d), lambda b, i: (l, b * nt + i, 0)),
        ],
        out_specs=(tok(),
                   pl.BlockSpec((r1, dff), lambda b, i: (b * nt + i, 0)),
                   pl.BlockSpec((r2, d), lambda b, i: (b * nt + i, 0))),
        compiler_params=_cparams(("parallel", "parallel")),
        name="merge_out",
    )(x, ret, lru, mls, proj, proj, proj, mod, w_br_ret, w_br_lru, w_br_mlstm, w_out, w_ff1, w_ff2)


def _ffn_kernel(x_ref, mod_ref, gain_ref, w1_ref, w2_ref, *rest, last):
    x = x_ref[0]
    h = _modulated_norm(x, gain_ref[0], mod_ref[0, 0, 4:5, :], mod_ref[0, 0, 3:4, :])
    a = jnp.square(jnp.maximum(_dot(h.astype(BF16), w1_ref[...]), 0.0))
    y = x + mod_ref[0, 0, 5:6, :] * _dot(a.astype(BF16), w2_ref[...])
    if last:
        fgain_ref, o_ref = rest
        o_ref[0] = y * lax.rsqrt(jnp.mean(y * y, axis=-1, keepdims=True) + EPS) * fgain_ref[...]
    else:
        nmod_ref, ngain_ref, o_ref, hn_ref = rest
        o_ref[0] = y
        hn = _modulated_norm(y, ngain_ref[0], nmod_ref[0, 0, 1:2, :], nmod_ref[0, 0, 0:1, :])
        hn_ref[0] = hn.astype(BF16)


def _ffn(x, mod, gain, w1, w2, mix_gain, final_gain, l, last):
    bsz, seq, d = x.shape
    dff = w1.shape[-1]
    tm = min(seq, 512)
    tok = lambda: pl.BlockSpec((1, tm, d), lambda b, i: (b, i, 0))
    in_specs = [
        tok(),
        pl.BlockSpec((1, 1, 6, d), lambda b, i: (l, b, 0, 0)),
        pl.BlockSpec((1, 1, d), lambda b, i: (l, 0, 0)),
        pl.BlockSpec((d, dff), lambda b, i: (0, 0), pipeline_mode=pl.Buffered(1)),
        pl.BlockSpec((dff, d), lambda b, i: (0, 0), pipeline_mode=pl.Buffered(1)),
    ]
    if last:
        in_specs.append(pl.BlockSpec((1, d), lambda b, i: (0, 0)))
        extra = (final_gain,)
        out_shape = jax.ShapeDtypeStruct((bsz, seq, d), F32)
        out_specs = tok()
    else:
        in_specs += [pl.BlockSpec((1, 1, 6, d), lambda b, i: (l + 1, b, 0, 0)),
                     pl.BlockSpec((1, 1, d), lambda b, i: (l + 1, 0, 0))]
        extra = (mod, mix_gain)
        out_shape = (jax.ShapeDtypeStruct((bsz, seq, d), F32), jax.ShapeDtypeStruct((bsz, seq, d), BF16))
        out_specs = (tok(), tok())
    return pl.pallas_call(
        functools.partial(_ffn_kernel, last=last),
        out_shape=out_shape,
        grid=(bsz, seq // tm),
        in_specs=in_specs,
        out_specs=out_specs,
        compiler_params=_cparams(("parallel", "parallel")),
        name="ffn",
    )(x, mod, gain, w1, w2, *extra)


def _block_diag_tiles(w, tile):
    depth, nb, bs, _ = w.shape
    rows = nb * bs
    sel = jnp.tile(jnp.eye(bs, dtype=w.dtype), (1, tile // bs))
    dense = jnp.einsum('lre,ec->lrc', w.reshape(depth, rows, bs), sel, precision=lax.Precision.HIGHEST)
    r = (np.arange(rows) % tile) // bs
    c = np.arange(tile) // bs
    dense = jnp.where(jnp.asarray(r[:, None] == c[None, :]), dense, 0.0)
    return dense.reshape(depth, rows // tile, tile, tile)


def kernel(x, c, positions, w_ada, b_ada, norm_mix, norm_mlp, w_in, lru_conv_w, lru_conv_b, lru_w_r, lru_b_r, lru_w_i, lru_b_i, lru_lambda, m_conv_w, m_conv_b, m_w_q, m_w_k, m_w_v, m_w_if, m_b_if, m_norm, w_br_ret, w_br_lru, w_br_mlstm, w_out, w_ff1, w_ff2, final_norm):
    depth = w_in.shape[0]
    bsz, seq, d = x.shape
    nh, dh = MLSTM_HEADS, MLSTM_DH

    mod = _ada(c, w_ada, b_ada).reshape(depth, bsz, 6, d)
    cc, ss = _rope_tables(positions)

    vec = lambda a: a.reshape(depth, 1, a.shape[-1])
    w_r_t = _block_diag_tiles(lru_w_r, V7X_LANES).astype(BF16)
    w_i_t = _block_diag_tiles(lru_w_i, V7X_LANES).astype(BF16)
    wq_t = _block_diag_tiles(m_w_q, dh).astype(BF16)
    wk_t = _block_diag_tiles(m_w_k, dh).astype(BF16)
    wv_t = _block_diag_tiles(m_w_v, dh).astype(BF16)
    wif_t = jnp.pad(m_w_if, ((0, 0), (0, 0), (0, V7X_LANES - 2 * nh))).reshape(depth, 3 * nh, dh, V7X_LANES).astype(BF16)
    bif_t = jnp.pad(m_b_if, ((0, 0), (0, V7X_LANES - 2 * nh))).reshape(depth, 1, V7X_LANES)
    g_mix, g_mlp = vec(norm_mix), vec(norm_mlp)
    l_cb, l_br, l_bi, l_lam = vec(lru_conv_b), vec(lru_b_r), vec(lru_b_i), vec(lru_lambda)
    m_cb, m_nrm = vec(m_conv_b), vec(m_norm)
    f_gain = final_norm.reshape(1, d)

    h = _prenorm(x, mod, g_mix, 0)
    for l in range(depth):
        proj = _inproj(h, w_in, l)
        ret = _retention(proj, cc, ss)
        lru = _lru(proj, lru_conv_w, l_cb, w_r_t, l_br, w_i_t, l_bi, l_lam, l)
        mq, mk, mv, gates, w_br_ret_b, w_br_lru_b, w_br_mls_b, w_out_b = _mlstm_prep(
            proj, m_conv_w, m_cb, wq_t, wk_t, wv_t, wif_t, bif_t, (w_br_ret, w_br_lru, w_br_mlstm, w_out), l)
        mls = _mlstm(mq, mk, mv, gates, proj, m_nrm, l)
        x, w_ff1_b, w_ff2_b = _merge(x, ret, lru, mls, proj, mod, w_br_ret_b, w_br_lru_b, w_br_mls_b, w_out_b,
                                     w_ff1, w_ff2, l)
        if l == depth - 1:
            return _ffn(x, mod, g_mlp, w_ff1_b, w_ff2_b, g_mix, f_gain, l, last=True)
        x, h = _ffn(x, mod, g_mlp, w_ff1_b, w_ff2_b, g_mix, f_gain, l, last=False)
```

```python
import functools

import jax
import jax.numpy as jnp
import numpy as np
from jax import lax
from jax.experimental import pallas as pl
from jax.experimental.pallas import tpu as pltpu

F32 = jnp.float32
BF16 = jnp.bfloat16

D_MODEL = 1024
RET_HEADS = 4
RET_DK = 128
RET_DV = 256
ROPE_BASE = 10000.0
LRU_C = 8.0
CONV_W = 4
MLSTM_HEADS = 4
MLSTM_DH = 256
CHUNK = 128
EPS = 1e-6

OFF_RQ, OFF_RK, OFF_RV, OFF_RG = 0, 512, 1024, 2048
OFF_LX, OFF_LY, OFF_MX, OFF_MO, OFF_GATE = 3072, 4096, 5120, 6144, 7168

V7X_LANES = 128
V7X_SUBLANES = 8
V7X_VMEM_LIMIT_BYTES = 56 * 1024 * 1024


def _cparams(sem):
    return pltpu.CompilerParams(dimension_semantics=sem, vmem_limit_bytes=V7X_VMEM_LIMIT_BYTES)


def _dot(a, b):
    return jnp.dot(a, b, preferred_element_type=F32)


def _dot_nt(a, b):
    return lax.dot_general(a, b, (((1,), (1,)), ((), ())), preferred_element_type=F32)


def _dot_tn(a, b):
    return lax.dot_general(a, b, (((0,), (0,)), ((), ())), preferred_element_type=F32)


def _sigmoid(x):
    return 0.5 * jnp.tanh(0.5 * x) + 0.5


def _silu(x):
    return x * _sigmoid(x)


def _ada_kernel(c_ref, w_ref, b_ref, o_ref):
    cond = _silu(c_ref[...])
    o_ref[0] = _dot(cond.astype(BF16), w_ref[0].astype(BF16)) + b_ref[0]


def _ada(c, w_ada, b_ada):
    depth, d, n = w_ada.shape
    bsz = c.shape[0]
    tn = 2048
    return pl.pallas_call(
        _ada_kernel,
        out_shape=jax.ShapeDtypeStruct((depth, bsz, n), F32),
        grid=(depth, n // tn),
        in_specs=[
            pl.BlockSpec((bsz, d), lambda l, j: (0, 0)),
            pl.BlockSpec((1, d, tn), lambda l, j: (l, 0, j)),
            pl.BlockSpec((1, 1, tn), lambda l, j: (l, 0, j)),
        ],
        out_specs=pl.BlockSpec((1, bsz, tn), lambda l, j: (l, 0, j)),
        compiler_params=_cparams(("parallel", "parallel")),
        name="ada_mod",
    )(c, w_ada, b_ada.reshape(depth, 1, n))


def _rope_kernel(pos_ref, invf_ref, cc_ref, ss_ref):
    ang = pos_ref[0] * invf_ref[...]
    lane = lax.broadcasted_iota(jnp.int32, ang.shape, 1)
    sn = jnp.sin(ang)
    cc_ref[0] = jnp.cos(ang)
    ss_ref[0] = jnp.where(lane < RET_DK // 2, -sn, sn)


def _rope_tables(positions):
    bsz, seq = positions.shape
    half = RET_DK // 2
    inv_freq = ROPE_BASE ** (-jnp.arange(half, dtype=F32) / half)
    invf = jnp.concatenate([inv_freq, inv_freq]).reshape(1, RET_DK)
    pos = positions.astype(F32).reshape(bsz, seq, 1)
    ts = min(seq, 512)
    out = jax.ShapeDtypeStruct((bsz, seq, RET_DK), F32)
    return pl.pallas_call(
        _rope_kernel,
        out_shape=(out, out),
        grid=(bsz, seq // ts),
        in_specs=[
            pl.BlockSpec((1, ts, 1), lambda b, t: (b, t, 0)),
            pl.BlockSpec((1, RET_DK), lambda b, t: (0, 0)),
        ],
        out_specs=(pl.BlockSpec((1, ts, RET_DK), lambda b, t: (b, t, 0)),
                   pl.BlockSpec((1, ts, RET_DK), lambda b, t: (b, t, 0))),
        compiler_params=_cparams(("parallel", "parallel")),
        name="rope_tables",
    )(pos, invf)


def _modulated_norm(x, gain, scale, shift):
    xn = x * lax.rsqrt(jnp.mean(x * x, axis=-1, keepdims=True) + EPS)
    return xn * gain * (1.0 + scale) + shift


def _prenorm_kernel(x_ref, mod_ref, gain_ref, o_ref):
    h = _modulated_norm(x_ref[0], gain_ref[0], mod_ref[0, 0, 1:2, :], mod_ref[0, 0, 0:1, :])
    o_ref[0] = h.astype(BF16)


def _prenorm(x, mod, gain, l):
    bsz, seq, d = x.shape
    tm = min(seq, 2048)
    tok = lambda: pl.BlockSpec((1, tm, d), lambda b, i: (b, i, 0))
    return pl.pallas_call(
        _prenorm_kernel,
        out_shape=jax.ShapeDtypeStruct((bsz, seq, d), BF16),
        grid=(bsz, seq // tm),
        in_specs=[
            tok(),
            pl.BlockSpec((1, 1, 6, d), lambda b, i: (l, b, 0, 0)),
            pl.BlockSpec((1, 1, d), lambda b, i: (l, 0, 0)),
        ],
        out_specs=tok(),
        compiler_params=_cparams(("parallel", "parallel")),
        name="prenorm",
    )(x, mod, gain)


INPROJ_N_SPLIT = 2


def _inproj_kernel(h_ref, w_ref, o_ref, wb_ref):
    @pl.when((pl.program_id(1) == 0) & (pl.program_id(2) == 0))
    def _():
        wb_ref[...] = w_ref[0].astype(BF16)

    part = wb_ref.shape[1] // INPROJ_N_SPLIT
    for n in range(INPROJ_N_SPLIT):
        cs = slice(n * part, (n + 1) * part)
        o_ref[0, :, cs] = _dot(h_ref[0], wb_ref[:, cs]).astype(BF16)


def _inproj(h, w_in, l):
    bsz, seq, d = h.shape
    n = w_in.shape[-1]
    tm = min(seq, 2048)
    tn = 2048
    return pl.pallas_call(
        _inproj_kernel,
        out_shape=jax.ShapeDtypeStruct((bsz, seq, n), BF16),
        grid=(n // tn, bsz, seq // tm),
        in_specs=[
            pl.BlockSpec((1, tm, d), lambda j, b, i: (b, i, 0)),
            pl.BlockSpec((1, d, tn), lambda j, b, i: (l, 0, j)),
        ],
        out_specs=pl.BlockSpec((1, tm, tn), lambda j, b, i: (b, i, j)),
        scratch_shapes=[pltpu.VMEM((d, tn), BF16)],
        compiler_params=_cparams(("arbitrary", "arbitrary", "arbitrary")),
        name="in_proj",
    )(h, w_in)


def _ret_kernel(q_ref, k_ref, v_ref, g_ref, cc_ref, ss_ref, o_ref, state_ref):
    head = pl.program_id(1)

    @pl.when(pl.program_id(2) == 0)
    def _():
        state_ref[...] = jnp.zeros_like(state_ref)

    hv = jnp.zeros((1, 1), F32) + head.astype(F32)
    log_gamma = jnp.log1p(-jnp.exp2(-5.0 - hv))
    ii = lax.broadcasted_iota(jnp.int32, (CHUNK, CHUNK), 0)
    jj = lax.broadcasted_iota(jnp.int32, (CHUNK, CHUNK), 1)
    causal = ii >= jj
    diff = jnp.where(causal, (ii - jj).astype(F32), 0.0)
    decay_intra = jnp.where(causal, jnp.exp(log_gamma * diff), 0.0)
    pos = lax.broadcasted_iota(jnp.int32, (CHUNK, 1), 0).astype(F32)
    decay_q = jnp.exp(log_gamma * (pos + 1.0))
    decay_k = jnp.exp(log_gamma * (CHUNK - 1.0 - pos))
    decay_chunk = jnp.exp(log_gamma * CHUNK)

    cc = cc_ref[0]
    ss = ss_ref[0]
    q = q_ref[0].astype(F32)
    k = k_ref[0].astype(F32)
    q = q * cc + pltpu.roll(q, RET_DK // 2, 1) * ss
    k = (k * cc + pltpu.roll(k, RET_DK // 2, 1) * ss) * RET_DK ** -0.5

    ts = q.shape[0]
    nc = ts // CHUNK
    qb = q.astype(BF16)
    kb = k.astype(BF16)
    qd = (q.reshape(nc, CHUNK, RET_DK) * decay_q).astype(BF16)
    kd = (k.reshape(nc, CHUNK, RET_DK) * decay_k).astype(BF16)
    v = v_ref[0]
    chunks = [slice(c * CHUNK, (c + 1) * CHUNK) for c in range(nc)]
    scores = [_dot_nt(qb[sl], kb[sl]) for sl in chunks]
    kv = [_dot_tn(kd[c], v[sl]) for c, sl in enumerate(chunks)]
    scores = [(s * decay_intra).astype(BF16) for s in scores]
    intra = [_dot(scores[c], v[sl]) for c, sl in enumerate(chunks)]
    state = state_ref[...]
    states = []
    for c in range(nc):
        states.append(state.astype(BF16))
        state = decay_chunk * state + kv[c]
    state_ref[...] = state
    inter = [_dot(qd[c], states[c]) for c in range(nc)]
    out = jnp.concatenate(intra, axis=0) + jnp.concatenate(inter, axis=0)
    out = out * lax.rsqrt(jnp.mean(out * out, axis=-1, keepdims=True) + EPS)
    o_ref[0] = (out * _silu(g_ref[0].astype(F32))).astype(BF16)


def _retention(proj, cc, ss):
    bsz, seq, _ = proj.shape
    ts = min(seq, 2048)
    qb, kb = OFF_RQ // RET_DK, OFF_RK // RET_DK
    vb, gb = OFF_RV // RET_DV, OFF_RG // RET_DV
    return pl.pallas_call(
        _ret_kernel,
        out_shape=jax.ShapeDtypeStruct((bsz, seq, RET_HEADS * RET_DV), BF16),
        grid=(bsz, RET_HEADS, seq // ts),
        in_specs=[
            pl.BlockSpec((1, ts, RET_DK), lambda b, h, t: (b, t, qb + h)),
            pl.BlockSpec((1, ts, RET_DK), lambda b, h, t: (b, t, kb + h)),
            pl.BlockSpec((1, ts, RET_DV), lambda b, h, t: (b, t, vb + h)),
            pl.BlockSpec((1, ts, RET_DV), lambda b, h, t: (b, t, gb + h)),
            pl.BlockSpec((1, ts, RET_DK), lambda b, h, t: (b, t, 0)),
            pl.BlockSpec((1, ts, RET_DK), lambda b, h, t: (b, t, 0)),
        ],
        out_specs=pl.BlockSpec((1, ts, RET_DV), lambda b, h, t: (b, t, h)),
        scratch_shapes=[pltpu.VMEM((RET_DK, RET_DV), F32)],
        compiler_params=_cparams(("parallel", "parallel", "arbitrary")),
        name="retention",
    )(proj, proj, proj, proj, cc, ss)


def _causal_conv(xs_ref, rows, w, b):
    y = b
    for k in range(CONV_W):
        s = CONV_W - 1 - k
        y = y + w[k:k + 1, :] * xs_ref[pl.ds(V7X_SUBLANES - s, rows), :]
    return y


LRU_COLS = 512
LRU_SEGS = V7X_SUBLANES


def _lru_pitch(seg_len):
    assert seg_len % V7X_SUBLANES == 0
    return seg_len + 4


def _lru_kernel(lx_ref, ly_ref, cw_ref, cb_ref, wr_ref, br_ref, wi_ref, bi_ref, lam_ref, o_ref,
                xs_ref, a_ref, u_ref, h_ref, p_ref):
    seq = lx_ref.shape[1]
    nslab = lx_ref.shape[2] // V7X_LANES
    seg_len = seq // LRU_SEGS
    pitch = _lru_pitch(seg_len)
    zeros8 = jnp.zeros((V7X_SUBLANES, V7X_LANES), F32)

    for s in range(nslab):
        cs = slice(s * V7X_LANES, (s + 1) * V7X_LANES)
        xs_ref[s, pl.ds(0, V7X_SUBLANES), :] = zeros8
        xs_ref[s, pl.ds(V7X_SUBLANES, seq), :] = lx_ref[0, :, cs].astype(F32)
        xl = _causal_conv(xs_ref.at[s], seq, cw_ref[0, :, cs], cb_ref[0, :, cs])
        xb = xl.astype(BF16)
        tr = jnp.tanh(0.5 * (_dot(xb, wr_ref[0, s]) + br_ref[0, :, cs]))
        ti = jnp.tanh(0.5 * (_dot(xb, wi_ref[0, s]) + bi_ref[0, :, cs]))
        lam = lam_ref[0, :, cs]
        softplus_neg_lam = jnp.maximum(-lam, 0.0) + jnp.log1p(jnp.exp(-jnp.abs(lam)))
        log_a = (tr + 1.0) * (-0.5 * LRU_C * softplus_neg_lam)
        a = jnp.exp(log_a)
        y = -jnp.tanh(log_a) * (a * a + 1.0)
        u = jnp.where(y > 0.0, y * lax.rsqrt(y), 0.0) * ((0.5 * ti + 0.5) * xl)
        for g in range(LRU_SEGS):
            a_ref[s, pl.ds(g * pitch, seg_len), :] = a[g * seg_len:(g + 1) * seg_len]
            u_ref[s, pl.ds(g * pitch, seg_len), :] = u[g * seg_len:(g + 1) * seg_len]

    def body(t, carry):
        hs, ps = carry
        new_h, new_p = [], []
        for s in range(nslab):
            rows = pl.ds(t, LRU_SEGS, stride=pitch)
            av = a_ref[s, rows, :]
            h = av * hs[s] + u_ref[s, rows, :]
            p = av * ps[s]
            h_ref[s, rows, :] = h
            p_ref[s, rows, :] = p
            new_h.append(h)
            new_p.append(p)
        return tuple(new_h), tuple(new_p)

    init = (tuple(zeros8 for _ in range(nslab)), tuple(zeros8 + 1.0 for _ in range(nslab)))
    h_end, p_end = lax.fori_loop(0, seg_len, body, init, unroll=8)

    for s in range(nslab):
        cs = slice(s * V7X_LANES, (s + 1) * V7X_LANES)
        carry = jnp.zeros((1, V7X_LANES), F32)
        for g in range(LRU_SEGS):
            rows = pl.ds(g * pitch, seg_len)
            h = h_ref[s, rows, :] + p_ref[s, rows, :] * carry
            gate = jax.nn.gelu(ly_ref[0, g * seg_len:(g + 1) * seg_len, cs].astype(F32))
            o_ref[0, g * seg_len:(g + 1) * seg_len, cs] = (h * gate).astype(BF16)
            carry = h_end[s][g:g + 1] + p_end[s][g:g + 1] * carry


def _lru(proj, conv_w, conv_b, w_r, b_r, w_i, b_i, lam, l):
    bsz, seq, _ = proj.shape
    w = LRU_COLS
    nslab = w // V7X_LANES
    scan_rows = LRU_SEGS * _lru_pitch(seq // LRU_SEGS)
    vec = lambda: pl.BlockSpec((1, 1, w), lambda b, j: (l, 0, j))
    wsp = lambda: pl.BlockSpec((1, nslab, V7X_LANES, V7X_LANES), lambda b, j: (l, j, 0, 0))
    return pl.pallas_call(
        _lru_kernel,
        out_shape=jax.ShapeDtypeStruct((bsz, seq, D_MODEL), BF16),
        grid=(bsz, D_MODEL // w),
        in_specs=[
            pl.BlockSpec((1, seq, w), lambda b, j: (b, 0, OFF_LX // w + j)),
            pl.BlockSpec((1, seq, w), lambda b, j: (b, 0, OFF_LY // w + j)),
            pl.BlockSpec((1, CONV_W, w), lambda b, j: (l, 0, j)),
            vec(), wsp(), vec(), wsp(), vec(), vec(),
        ],
        out_specs=pl.BlockSpec((1, seq, w), lambda b, j: (b, 0, j)),
        scratch_shapes=[pltpu.VMEM((nslab, seq + V7X_SUBLANES, V7X_LANES), F32)]
                       + [pltpu.VMEM((nslab, scan_rows, V7X_LANES), F32)] * 4,
        compiler_params=_cparams(("parallel", "parallel")),
        name="rg_lru",
    )(proj, proj, conv_w, conv_b, w_r, b_r, w_i, b_i, lam)


def _mprep_kernel(mx_ref, cw_ref, cb_ref, wq_ref, wk_ref, wv_ref, wif_ref, bif_ref, *rest):
    wf_refs, (q_ref, k_ref, v_ref, gate_ref), wb_refs, xs_ref = rest[:4], rest[4:8], rest[8:12], rest[12]
    for wf_ref, wb_ref in zip(wf_refs, wb_refs):
        wb_ref[...] = wf_ref[0].astype(BF16)
    ts = mx_ref.shape[1]

    @pl.when(pl.program_id(1) == 0)
    def _():
        for s in range(xs_ref.shape[0]):
            xs_ref[s, pl.ds(ts, V7X_SUBLANES), :] = jnp.zeros((V7X_SUBLANES, V7X_LANES), F32)

    mxb = mx_ref[0]
    xc = []
    for s in range(xs_ref.shape[0]):
        cs = slice(s * V7X_LANES, (s + 1) * V7X_LANES)
        xs_ref[s, pl.ds(0, V7X_SUBLANES), :] = xs_ref[s, pl.ds(ts, V7X_SUBLANES), :]
        xs_ref[s, pl.ds(V7X_SUBLANES, ts), :] = mxb[:, cs].astype(F32)
        y = _causal_conv(xs_ref.at[s], ts, cw_ref[0, :, cs], cb_ref[0, :, cs])
        xc.append(_silu(y).astype(BF16))
    xc = jnp.concatenate(xc, axis=1)
    acc = jnp.zeros((ts, V7X_LANES), F32) + bif_ref[0]
    nh = MLSTM_HEADS
    for h in range(nh):
        cs = slice(h * MLSTM_DH, (h + 1) * MLSTM_DH)
        mq = _dot(xc[:, cs], wq_ref[0, h]).astype(BF16)
        mk = _dot(xc[:, cs], wk_ref[0, h]).astype(BF16)
        mv = _dot(mxb[:, cs], wv_ref[0, h]).astype(BF16)
        q_ref[0, :, cs] = mq
        k_ref[0, :, cs] = mk
        v_ref[0, :, cs] = mv
        acc = acc + _dot(mq, wif_ref[0, h]) + _dot(mk, wif_ref[0, nh + h]) + _dot(mv, wif_ref[0, 2 * nh + h])
    lane = lax.broadcasted_iota(jnp.int32, acc.shape, 1)
    log_f = jnp.minimum(acc, 0.0) - jnp.log1p(jnp.exp(-jnp.abs(acc)))
    gate_ref[0] = jnp.where(lane >= nh, log_f, acc)


def _mlstm_prep(proj, conv_w, conv_b, wq, wk, wv, wif, bif, merge_ws, l):
    bsz, seq, _ = proj.shape
    width = MLSTM_HEADS * MLSTM_DH
    ts = min(seq, 512)
    nt = seq // ts
    nh, dh = MLSTM_HEADS, MLSTM_DH
    d = merge_ws[0].shape[-1]
    rows = merge_ws[0].shape[1] // (bsz * nt)
    qkv = jax.ShapeDtypeStruct((bsz, seq, width), BF16)
    wb_shape = jax.ShapeDtypeStruct(merge_ws[0].shape[1:], BF16)
    wspec = lambda: pl.BlockSpec((1, nh, dh, dh), lambda b, t: (l, 0, 0, 0))
    ospec = lambda: pl.BlockSpec((1, ts, width), lambda b, t: (b, t, 0))
    return pl.pallas_call(
        _mprep_kernel,
        out_shape=(qkv, qkv, qkv, jax.ShapeDtypeStruct((bsz, seq, V7X_LANES), F32)) + (wb_shape,) * 4,
        grid=(bsz, nt),
        in_specs=[
            pl.BlockSpec((1, ts, width), lambda b, t: (b, t, OFF_MX // width)),
            pl.BlockSpec((1, CONV_W, width), lambda b, t: (l, 0, 0)),
            pl.BlockSpec((1, 1, width), lambda b, t: (l, 0, 0)),
            wspec(), wspec(), wspec(),
            pl.BlockSpec((1, 3 * nh, dh, V7X_LANES), lambda b, t: (l, 0, 0, 0)),
            pl.BlockSpec((1, 1, V7X_LANES), lambda b, t: (l, 0, 0)),
        ] + [pl.BlockSpec((1, rows, d), lambda b, t: (l, b * nt + t, 0))] * 4,
        out_specs=(ospec(), ospec(), ospec(),
                   pl.BlockSpec((1, ts, V7X_LANES), lambda b, t: (b, t, 0)))
                  + (pl.BlockSpec((rows, d), lambda b, t: (b * nt + t, 0)),) * 4,
        scratch_shapes=[pltpu.VMEM((width // V7X_LANES, ts + V7X_SUBLANES, V7X_LANES), F32)],
        compiler_params=_cparams(("parallel", "arbitrary")),
        name="mlstm_prep",
    )(proj, conv_w, conv_b, wq, wk, wv, wif, bif, *merge_ws)


def _mlstm_kernel(q_ref, k_ref, v_ref, gate_ref, mo_ref, mn_ref, o_ref, c_ref, n_ref, m_ref):
    head = pl.program_id(1)

    @pl.when(pl.program_id(2) == 0)
    def _():
        c_ref[...] = jnp.zeros_like(c_ref)
        n_ref[...] = jnp.zeros_like(n_ref)
        m_ref[...] = jnp.zeros_like(m_ref)

    gates = gate_ref[0]
    ts = gates.shape[0]
    nc = ts // CHUNK
    lane = lax.broadcasted_iota(jnp.int32, gates.shape, 1)
    ic = jnp.sum(jnp.where(lane == head, gates, 0.0), axis=1, keepdims=True).reshape(nc, CHUNK, 1)
    lf = jnp.sum(jnp.where(lane == head + MLSTM_HEADS, gates, 0.0), axis=1, keepdims=True).reshape(nc, CHUNK, 1)

    ii = lax.broadcasted_iota(jnp.int32, (nc, CHUNK, CHUNK), 1)
    jj = lax.broadcasted_iota(jnp.int32, (nc, CHUNK, CHUNK), 2)
    causal = ii >= jj
    diag = ii == jj
    b_row = jnp.sum(jnp.where(ii <= jj, lf, 0.0), axis=1, keepdims=True)
    lf_row = jnp.sum(jnp.where(diag, lf, 0.0), axis=1, keepdims=True)
    ic_row = jnp.sum(jnp.where(diag, ic, 0.0), axis=1, keepdims=True)
    b_col = jnp.sum(jnp.where(causal, lf_row, 0.0), axis=2, keepdims=True)
    dmat = jnp.where(causal, b_col - b_row + ic_row, -jnp.inf)
    row_max = jnp.max(dmat, axis=2, keepdims=True)

    m_s = m_ref[...]
    m_t_list, m_prev_list = [], []
    for c in range(nc):
        m_prev_list.append(m_s)
        m_tc = jnp.maximum(b_col[c] + m_s, row_max[c])
        m_t_list.append(m_tc)
        m_s = m_tc[CHUNK - 1:]
    m_ref[...] = m_s

    q = q_ref[0]
    v = v_ref[0]
    kb = k_ref[0] * jnp.asarray(MLSTM_DH ** -0.5, BF16)
    m_t = jnp.stack(m_t_list)
    m_prev = jnp.stack(m_prev_list)
    w_inter = jnp.exp(b_col + m_prev - m_t)
    b_last = b_col[:, CHUNK - 1:, :]
    m_new = m_t[:, CHUNK - 1:, :]
    w_k = jnp.exp(b_last - b_col + ic - m_new)
    decay = jnp.exp(b_last + m_prev - m_new)
    p = jnp.exp(dmat - m_t)
    kwb = kb.reshape(nc, CHUNK, MLSTM_DH) * w_k.astype(BF16)
    ones = jnp.ones((V7X_SUBLANES, CHUNK), BF16)

    chunks = [slice(c * CHUNK, (c + 1) * CHUNK) for c in range(nc)]
    qk = [_dot_nt(q[sl], kb[sl]) for sl in chunks]
    kv = [_dot_tn(kwb[c], v[sl]) for c, sl in enumerate(chunks)]
    n_add = [_dot(ones, kwb[c]) for c in range(nc)]
    s = [qk[c] * p[c] for c in range(nc)]
    intra = [_dot(s[c].astype(BF16), v[sl]) for c, sl in enumerate(chunks)]
    c_s = c_ref[...]
    n_s = n_ref[...]
    c_states, n_states = [], []
    for c in range(nc):
        c_states.append(c_s.astype(BF16))
        n_states.append(n_s)
        c_s = decay[c] * c_s + kv[c]
        n_s = decay[c] * n_s + n_add[c][:1]
    c_ref[...] = c_s
    n_ref[...] = n_s
    inter = [_dot(q[sl], c_states[c]) for c, sl in enumerate(chunks)]
    w_inter = w_inter.reshape(ts, 1)
    num = jnp.concatenate(intra, axis=0) + w_inter * jnp.concatenate(inter, axis=0)
    s_sum = jnp.sum(jnp.stack(s), axis=2, keepdims=True).reshape(ts, 1)
    qn = [_dot_nt(q[sl], jnp.broadcast_to(n_states[c], (V7X_SUBLANES, MLSTM_DH)).astype(BF16))[:, :1]
          for c, sl in enumerate(chunks)]
    den = s_sum + w_inter * jnp.concatenate(qn, axis=0)
    h = num / jnp.maximum(jnp.abs(den), jnp.exp(-m_t.reshape(ts, 1)))
    o = _sigmoid(mo_ref[0].astype(F32)) * h
    o = o * lax.rsqrt(jnp.mean(o * o, axis=-1, keepdims=True) + EPS)
    o_ref[0] = (o * mn_ref[0]).astype(BF16)


def _mlstm(mq, mk, mv, gates, proj, m_norm, l):
    bsz, seq, width = mq.shape
    dh = MLSTM_DH
    ts = min(seq, 2048)
    hspec = lambda: pl.BlockSpec((1, ts, dh), lambda b, h, t: (b, t, h))
    return pl.pallas_call(
        _mlstm_kernel,
        out_shape=jax.ShapeDtypeStruct((bsz, seq, width), BF16),
        grid=(bsz, MLSTM_HEADS, seq // ts),
        in_specs=[
            hspec(), hspec(), hspec(),
            pl.BlockSpec((1, ts, V7X_LANES), lambda b, h, t: (b, t, 0)),
            pl.BlockSpec((1, ts, dh), lambda b, h, t: (b, t, OFF_MO // dh + h)),
            pl.BlockSpec((1, 1, dh), lambda b, h, t: (l, 0, h)),
        ],
        out_specs=hspec(),
        scratch_shapes=[pltpu.VMEM((dh, dh), F32), pltpu.VMEM((1, dh), F32), pltpu.VMEM((1, 1), F32)],
        compiler_params=_cparams(("parallel", "parallel", "arbitrary")),
        name="mlstm",
    )(mq, mk, mv, gates, proj, m_norm)


def _merge_kernel(x_ref, ret_ref, lru_ref, mls_ref, g0_ref, g1_ref, g2_ref, mod_ref,
                  wr_ref, wl_ref, wm_ref, wo_ref, w1f_ref, w2f_ref, o_ref, w1b_ref, w2b_ref):
    w1b_ref[...] = w1f_ref[0].astype(BF16)
    w2b_ref[...] = w2f_ref[0].astype(BF16)
    merged = (_sigmoid(g0_ref[0].astype(F32)) * _dot(ret_ref[0], wr_ref[...])
              + _sigmoid(g1_ref[0].astype(F32)) * _dot(lru_ref[0], wl_ref[...])
              + _sigmoid(g2_ref[0].astype(F32)) * _dot(mls_ref[0], wm_ref[...]))
    y = _dot(merged.astype(BF16), wo_ref[...])
    o_ref[0] = x_ref[0] + mod_ref[0, 0, 2:3, :] * y


def _merge(x, ret, lru, mls, proj, mod, w_br_ret, w_br_lru, w_br_mlstm, w_out, w_ff1, w_ff2, l):
    bsz, seq, d = x.shape
    dff = w_ff1.shape[-1]
    tm = min(seq, 512)
    nt = seq // tm
    steps = bsz * nt
    r1, r2 = d // steps, dff // steps
    gb = OFF_GATE // d
    tok = lambda: pl.BlockSpec((1, tm, d), lambda b, i: (b, i, 0))
    wsp = lambda: pl.BlockSpec((d, d), lambda b, i: (0, 0))
    return pl.pallas_call(
        _merge_kernel,
        out_shape=(jax.ShapeDtypeStruct((bsz, seq, d), F32),
                   jax.ShapeDtypeStruct((d, dff), BF16), jax.ShapeDtypeStruct((dff, d), BF16)),
        grid=(bsz, nt),
        in_specs=[
            tok(), tok(), tok(), tok(),
            pl.BlockSpec((1, tm, d), lambda b, i: (b, i, gb)),
            pl.BlockSpec((1, tm, d), lambda b, i: (b, i, gb + 1)),
            pl.BlockSpec((1, tm, d), lambda b, i: (b, i, gb + 2)),
            pl.BlockSpec((1, 1, 6, d), lambda b, i: (l, b, 0, 0)),
            wsp(), wsp(), wsp(), wsp(),
            pl.BlockSpec((1, r1, dff), lambda b, i: (l, b * nt + i, 0)),
            pl.BlockSpec((1, r2, d), lambda b, i: (l, b * nt + i, 0)),
        ],
        out_specs=(tok(),
                   pl.BlockSpec((r1, dff), lambda b, i: (b * nt + i, 0)),
                   pl.BlockSpec((r2, d), lambda b, i: (b * nt + i, 0))),
        compiler_params=_cparams(("parallel", "parallel")),
        name="merge_out",
    )(x, ret, lru, mls, proj, proj, proj, mod, w_br_ret, w_br_lru, w_br_mlstm, w_out, w_ff1, w_ff2)


def _ffn_kernel(x_ref, mod_ref, gain_ref, w1_ref, w2_ref, *rest, last):
    x = x_ref[0]
    h = _modulated_norm(x, gain_ref[0], mod_ref[0, 0, 4:5, :], mod_ref[0, 0, 3:4, :])
    a = jnp.square(jnp.maximum(_dot(h.astype(BF16), w1_ref[...]), 0.0))
    y = x + mod_ref[0, 0, 5:6, :] * _dot(a.astype(BF16), w2_ref[...])
    if last:
        fgain_ref, o_ref = rest
        o_ref[0] = y * lax.rsqrt(jnp.mean(y * y, axis=-1, keepdims=True) + EPS) * fgain_ref[...]
    else:
        nmod_ref, ngain_ref, o_ref, hn_ref = rest
        o_ref[0] = y
        hn = _modulated_norm(y, ngain_ref[0], nmod_ref[0, 0, 1:2, :], nmod_ref[0, 0, 0:1, :])
        hn_ref[0] = hn.astype(BF16)


def _ffn(x, mod, gain, w1, w2, mix_gain, final_gain, l, last):
    bsz, seq, d = x.shape
    dff = w1.shape[-1]
    tm = min(seq, 512)
    tok = lambda: pl.BlockSpec((1, tm, d), lambda b, i: (b, i, 0))
    in_specs = [
        tok(),
        pl.BlockSpec((1, 1, 6, d), lambda b, i: (l, b, 0, 0)),
        pl.BlockSpec((1, 1, d), lambda b, i: (l, 0, 0)),
        pl.BlockSpec((d, dff), lambda b, i: (0, 0), pipeline_mode=pl.Buffered(1)),
        pl.BlockSpec((dff, d), lambda b, i: (0, 0), pipeline_mode=pl.Buffered(1)),
    ]
    if last:
        in_specs.append(pl.BlockSpec((1, d), lambda b, i: (0, 0)))
        extra = (final_gain,)
        out_shape = jax.ShapeDtypeStruct((bsz, seq, d), F32)
        out_specs = tok()
    else:
        in_specs += [pl.BlockSpec((1, 1, 6, d), lambda b, i: (l + 1, b, 0, 0)),
                     pl.BlockSpec((1, 1, d), lambda b, i: (l + 1, 0, 0))]
        extra = (mod, mix_gain)
        out_shape = (jax.ShapeDtypeStruct((bsz, seq, d), F32), jax.ShapeDtypeStruct((bsz, seq, d), BF16))
        out_specs = (tok(), tok())
    return pl.pallas_call(
        functools.partial(_ffn_kernel, last=last),
        out_shape=out_shape,
        grid=(bsz, seq // tm),
        in_specs=in_specs,
        out_specs=out_specs,
        compiler_params=_cparams(("parallel", "parallel")),
        name="ffn",
    )(x, mod, gain, w1, w2, *extra)


def _block_diag_tiles(w, tile):
    depth, nb, bs, _ = w.shape
    rows = nb * bs
    sel = jnp.tile(jnp.eye(bs, dtype=w.dtype), (1, tile // bs))
    dense = jnp.einsum('lre,ec->lrc', w.reshape(depth, rows, bs), sel, precision=lax.Precision.HIGHEST)
    r = (np.arange(rows) % tile) // bs
    c = np.arange(tile) // bs
    dense = jnp.where(jnp.asarray(r[:, None] == c[None, :]), dense, 0.0)
    return dense.reshape(depth, rows // tile, tile, tile)


def kernel(x, c, positions, w_ada, b_ada, norm_mix, norm_mlp, w_in, lru_conv_w, lru_conv_b, lru_w_r, lru_b_r, lru_w_i, lru_b_i, lru_lambda, m_conv_w, m_conv_b, m_w_q, m_w_k, m_w_v, m_w_if, m_b_if, m_norm, w_br_ret, w_br_lru, w_br_mlstm, w_out, w_ff1, w_ff2, final_norm):
    depth = w_in.shape[0]
    bsz, seq, d = x.shape
    nh, dh = MLSTM_HEADS, MLSTM_DH

    mod = _ada(c, w_ada, b_ada).reshape(depth, bsz, 6, d)
    cc, ss = _rope_tables(positions)

    vec = lambda a: a.reshape(depth, 1, a.shape[-1])
    w_r_t = _block_diag_tiles(lru_w_r, V7X_LANES).astype(BF16)
    w_i_t = _block_diag_tiles(lru_w_i, V7X_LANES).astype(BF16)
    wq_t = _block_diag_tiles(m_w_q, dh).astype(BF16)
    wk_t = _block_diag_tiles(m_w_k, dh).astype(BF16)
    wv_t = _block_diag_tiles(m_w_v, dh).astype(BF16)
    wif_t = jnp.pad(m_w_if, ((0, 0), (0, 0), (0, V7X_LANES - 2 * nh))).reshape(depth, 3 * nh, dh, V7X_LANES).astype(BF16)
    bif_t = jnp.pad(m_b_if, ((0, 0), (0, V7X_LANES - 2 * nh))).reshape(depth, 1, V7X_LANES)
    g_mix, g_mlp = vec(norm_mix), vec(norm_mlp)
    l_cb, l_br, l_bi, l_lam = vec(lru_conv_b), vec(lru_b_r), vec(lru_b_i), vec(lru_lambda)
    m_cb, m_nrm = vec(m_conv_b), vec(m_norm)
    f_gain = final_norm.reshape(1, d)

    h = _prenorm(x, mod, g_mix, 0)
    for l in range(depth):
        proj = _inproj(h, w_in, l)
        ret = _retention(proj, cc, ss)
        lru = _lru(proj, lru_conv_w, l_cb, w_r_t, l_br, w_i_t, l_bi, l_lam, l)
        mq, mk, mv, gates, w_br_ret_b, w_br_lru_b, w_br_mls_b, w_out_b = _mlstm_prep(
            proj, m_conv_w, m_cb, wq_t, wk_t, wv_t, wif_t, bif_t, (w_br_ret, w_br_lru, w_br_mlstm, w_out), l)
        mls = _mlstm(mq, mk, mv, gates, proj, m_nrm, l)
        x, w_ff1_b, w_ff2_b = _merge(x, ret, lru, mls, proj, mod, w_br_ret_b, w_br_lru_b, w_br_mls_b, w_out_b,
                                     w_ff1, w_ff2, l)
        if l == depth - 1:
            return _ffn(x, mod, g_mlp, w_ff1_b, w_ff2_b, g_mix, f_gain, l, last=True)
        x, h = _ffn(x, mod, g_mlp, w_ff1_b, w_ff2_b, g_mix, f_gain, l, last=False)
```

```python
import functools

import jax
import jax.numpy as jnp
import numpy as np
from jax import lax
from jax.experimental import pallas as pl
from jax.experimental.pallas import tpu as pltpu

F32 = jnp.float32
BF16 = jnp.bfloat16

D_MODEL = 1024
RET_HEADS = 4
RET_DK = 128
RET_DV = 256
ROPE_BASE = 10000.0
LRU_C = 8.0
CONV_W = 4
MLSTM_HEADS = 4
MLSTM_DH = 256
CHUNK = 128
EPS = 1e-6

OFF_RQ, OFF_RK, OFF_RV, OFF_RG = 0, 512, 1024, 2048
OFF_LX, OFF_LY, OFF_MX, OFF_MO, OFF_GATE = 3072, 4096, 5120, 6144, 7168

V7X_LANES = 128
V7X_SUBLANES = 8
V7X_VMEM_LIMIT_BYTES = 56 * 1024 * 1024


def _cparams(sem):
    return pltpu.CompilerParams(dimension_semantics=sem, vmem_limit_bytes=V7X_VMEM_LIMIT_BYTES)


def _dot(a, b):
    return jnp.dot(a, b, preferred_element_type=F32)


def _dot_nt(a, b):
    return lax.dot_general(a, b, (((1,), (1,)), ((), ())), preferred_element_type=F32)


def _dot_tn(a, b):
    return lax.dot_general(a, b, (((0,), (0,)), ((), ())), preferred_element_type=F32)


def _sigmoid(x):
    return 0.5 * jnp.tanh(0.5 * x) + 0.5


def _silu(x):
    return x * _sigmoid(x)


def _ada_kernel(c_ref, w_ref, b_ref, o_ref):
    cond = _silu(c_ref[...])
    o_ref[0] = _dot(cond.astype(BF16), w_ref[0].astype(BF16)) + b_ref[0]


def _ada(c, w_ada, b_ada):
    depth, d, n = w_ada.shape
    bsz = c.shape[0]
    tn = 2048
    return pl.pallas_call(
        _ada_kernel,
        out_shape=jax.ShapeDtypeStruct((depth, bsz, n), F32),
        grid=(depth, n // tn),
        in_specs=[
            pl.BlockSpec((bsz, d), lambda l, j: (0, 0)),
            pl.BlockSpec((1, d, tn), lambda l, j: (l, 0, j)),
            pl.BlockSpec((1, 1, tn), lambda l, j: (l, 0, j)),
        ],
        out_specs=pl.BlockSpec((1, bsz, tn), lambda l, j: (l, 0, j)),
        compiler_params=_cparams(("parallel", "parallel")),
        name="ada_mod",
    )(c, w_ada, b_ada.reshape(depth, 1, n))


def _rope_kernel(pos_ref, invf_ref, cc_ref, ss_ref):
    half = RET_DK // 2
    rows = pos_ref.shape[1] // 2
    lo = lax.broadcasted_iota(jnp.int32, (rows, RET_DK), 1) < half
    ang = jnp.where(lo, pos_ref[0, :rows], pos_ref[0, rows:]) * invf_ref[...]
    c, s = jnp.cos(ang), jnp.sin(ang)
    c_sw, s_sw = pltpu.roll(c, half, 1), pltpu.roll(s, half, 1)
    cc_ref[0, :rows] = jnp.where(lo, c, c_sw)
    cc_ref[0, rows:] = jnp.where(lo, c_sw, c)
    ss_ref[0, :rows] = jnp.where(lo, -s, s_sw)
    ss_ref[0, rows:] = jnp.where(lo, -s_sw, s)


def _rope_tables(positions):
    bsz, seq = positions.shape
    half = RET_DK // 2
    inv_freq = ROPE_BASE ** (-jnp.arange(half, dtype=F32) / half)
    invf = jnp.concatenate([inv_freq, inv_freq]).reshape(1, RET_DK)
    pos = positions.astype(F32).reshape(bsz, seq, 1)
    ts = min(seq, 512)
    out = jax.ShapeDtypeStruct((bsz, seq, RET_DK), F32)
    return pl.pallas_call(
        _rope_kernel,
        out_shape=(out, out),
        grid=(bsz, seq // ts),
        in_specs=[
            pl.BlockSpec((1, ts, 1), lambda b, t: (b, t, 0)),
            pl.BlockSpec((1, RET_DK), lambda b, t: (0, 0)),
        ],
        out_specs=(pl.BlockSpec((1, ts, RET_DK), lambda b, t: (b, t, 0)),
                   pl.BlockSpec((1, ts, RET_DK), lambda b, t: (b, t, 0))),
        compiler_params=_cparams(("parallel", "parallel")),
        name="rope_tables",
    )(pos, invf)


def _modulated_norm(x, gain, scale, shift):
    xn = x * lax.rsqrt(jnp.mean(x * x, axis=-1, keepdims=True) + EPS)
    return xn * gain * (1.0 + scale) + shift


def _prenorm_kernel(x_ref, mod_ref, gain_ref, o_ref):
    h = _modulated_norm(x_ref[0], gain_ref[0], mod_ref[0, 0, 1:2, :], mod_ref[0, 0, 0:1, :])
    o_ref[0] = h.astype(BF16)


def _prenorm(x, mod, gain, l):
    bsz, seq, d = x.shape
    tm = min(seq, 2048)
    tok = lambda: pl.BlockSpec((1, tm, d), lambda b, i: (b, i, 0))
    return pl.pallas_call(
        _prenorm_kernel,
        out_shape=jax.ShapeDtypeStruct((bsz, seq, d), BF16),
        grid=(bsz, seq // tm),
        in_specs=[
            tok(),
            pl.BlockSpec((1, 1, 6, d), lambda b, i: (l, b, 0, 0)),
            pl.BlockSpec((1, 1, d), lambda b, i: (l, 0, 0)),
        ],
        out_specs=tok(),
        compiler_params=_cparams(("parallel", "parallel")),
        name="prenorm",
    )(x, mod, gain)


INPROJ_N_SPLIT = 2


def _inproj_kernel(h_ref, w_ref, o_ref, wb_ref):
    @pl.when((pl.program_id(1) == 0) & (pl.program_id(2) == 0))
    def _():
        wb_ref[...] = w_ref[0].astype(BF16)

    part = wb_ref.shape[1] // INPROJ_N_SPLIT
    for n in range(INPROJ_N_SPLIT):
        cs = slice(n * part, (n + 1) * part)
        o_ref[0, :, cs] = _dot(h_ref[0], wb_ref[:, cs]).astype(BF16)


def _inproj(h, w_in, l):
    bsz, seq, d = h.shape
    n = w_in.shape[-1]
    tm = min(seq, 2048)
    tn = 2048
    return pl.pallas_call(
        _inproj_kernel,
        out_shape=jax.ShapeDtypeStruct((bsz, seq, n), BF16),
        grid=(n // tn, bsz, seq // tm),
        in_specs=[
            pl.BlockSpec((1, tm, d), lambda j, b, i: (b, i, 0)),
            pl.BlockSpec((1, d, tn), lambda j, b, i: (l, 0, j)),
        ],
        out_specs=pl.BlockSpec((1, tm, tn), lambda j, b, i: (b, i, j)),
        scratch_shapes=[pltpu.VMEM((d, tn), BF16)],
        compiler_params=_cparams(("arbitrary", "arbitrary", "arbitrary")),
        name="in_proj",
    )(h, w_in)


def _ret_kernel(q_ref, k_ref, v_ref, g_ref, cc_ref, ss_ref, o_ref, state_ref):
    head = pl.program_id(1)

    @pl.when(pl.program_id(2) == 0)
    def _():
        state_ref[...] = jnp.zeros_like(state_ref)

    hv = jnp.zeros((1, 1), F32) + head.astype(F32)
    log_gamma = jnp.log1p(-jnp.exp2(-5.0 - hv))
    ii = lax.broadcasted_iota(jnp.int32, (CHUNK, CHUNK), 0)
    jj = lax.broadcasted_iota(jnp.int32, (CHUNK, CHUNK), 1)
    causal = ii >= jj
    diff = jnp.where(causal, (ii - jj).astype(F32), 0.0)
    decay_intra = jnp.where(causal, jnp.exp(log_gamma * diff), 0.0)
    pos = lax.broadcasted_iota(jnp.int32, (CHUNK, 1), 0).astype(F32)
    decay_q = jnp.exp(log_gamma * (pos + 1.0))
    decay_k = jnp.exp(log_gamma * (CHUNK - 1.0 - pos))
    decay_chunk = jnp.exp(log_gamma * CHUNK)

    cc = cc_ref[0]
    ss = ss_ref[0]
    q = q_ref[0].astype(F32)
    k = k_ref[0].astype(F32)
    q = q * cc + pltpu.roll(q, RET_DK // 2, 1) * ss
    k = (k * cc + pltpu.roll(k, RET_DK // 2, 1) * ss) * RET_DK ** -0.5

    ts = q.shape[0]
    nc = ts // CHUNK
    qb = q.astype(BF16)
    kb = k.astype(BF16)
    qd = (q.reshape(nc, CHUNK, RET_DK) * decay_q).astype(BF16)
    kd = (k.reshape(nc, CHUNK, RET_DK) * decay_k).astype(BF16)
    v = v_ref[0]
    chunks = [slice(c * CHUNK, (c + 1) * CHUNK) for c in range(nc)]
    scores = [_dot_nt(qb[sl], kb[sl]) for sl in chunks]
    kv = [_dot_tn(kd[c], v[sl]) for c, sl in enumerate(chunks)]
    scores = [(s * decay_intra).astype(BF16) for s in scores]
    intra = [_dot(scores[c], v[sl]) for c, sl in enumerate(chunks)]
    state = state_ref[...]
    states = []
    for c in range(nc):
        states.append(state.astype(BF16))
        state = decay_chunk * state + kv[c]
    state_ref[...] = state
    inter = [_dot(qd[c], states[c]) for c in range(nc)]
    out = jnp.concatenate(intra, axis=0) + jnp.concatenate(inter, axis=0)
    out = out * lax.rsqrt(jnp.mean(out * out, axis=-1, keepdims=True) + EPS)
    o_ref[0] = (out * _silu(g_ref[0].astype(F32))).astype(BF16)


def _retention(proj, cc, ss):
    bsz, seq, _ = proj.shape
    ts = min(seq, 2048)
    qb, kb = OFF_RQ // RET_DK, OFF_RK // RET_DK
    vb, gb = OFF_RV // RET_DV, OFF_RG // RET_DV
    return pl.pallas_call(
        _ret_kernel,
        out_shape=jax.ShapeDtypeStruct((bsz, seq, RET_HEADS * RET_DV), BF16),
        grid=(bsz, RET_HEADS, seq // ts),
        in_specs=[
            pl.BlockSpec((1, ts, RET_DK), lambda b, h, t: (b, t, qb + h)),
            pl.BlockSpec((1, ts, RET_DK), lambda b, h, t: (b, t, kb + h)),
            pl.BlockSpec((1, ts, RET_DV), lambda b, h, t: (b, t, vb + h)),
            pl.BlockSpec((1, ts, RET_DV), lambda b, h, t: (b, t, gb + h)),
            pl.BlockSpec((1, ts, RET_DK), lambda b, h, t: (b, t, 0)),
            pl.BlockSpec((1, ts, RET_DK), lambda b, h, t: (b, t, 0)),
        ],
        out_specs=pl.BlockSpec((1, ts, RET_DV), lambda b, h, t: (b, t, h)),
        scratch_shapes=[pltpu.VMEM((RET_DK, RET_DV), F32)],
        compiler_params=_cparams(("parallel", "parallel", "arbitrary")),
        name="retention",
    )(proj, proj, proj, proj, cc, ss)


def _causal_conv(xs_ref, rows, w, b):
    y = b
    for k in range(CONV_W):
        s = CONV_W - 1 - k
        y = y + w[k:k + 1, :] * xs_ref[pl.ds(V7X_SUBLANES - s, rows), :]
    return y


LRU_COLS = 512
LRU_SEGS = V7X_SUBLANES


def _lru_pitch(seg_len):
    assert seg_len % V7X_SUBLANES == 0
    return seg_len + 4


def _lru_kernel(lx_ref, ly_ref, cw_ref, cb_ref, wr_ref, br_ref, wi_ref, bi_ref, lam_ref, o_ref,
                xs_ref, a_ref, u_ref, h_ref, p_ref):
    seq = lx_ref.shape[1]
    nslab = lx_ref.shape[2] // V7X_LANES
    seg_len = seq // LRU_SEGS
    pitch = _lru_pitch(seg_len)
    zeros8 = jnp.zeros((V7X_SUBLANES, V7X_LANES), F32)

    for s in range(nslab):
        cs = slice(s * V7X_LANES, (s + 1) * V7X_LANES)
        xs_ref[s, pl.ds(0, V7X_SUBLANES), :] = zeros8
        xs_ref[s, pl.ds(V7X_SUBLANES, seq), :] = lx_ref[0, :, cs].astype(F32)
        xl = _causal_conv(xs_ref.at[s], seq, cw_ref[0, :, cs], cb_ref[0, :, cs])
        xb = xl.astype(BF16)
        tr = jnp.tanh(0.5 * (_dot(xb, wr_ref[0, s]) + br_ref[0, :, cs]))
        ti = jnp.tanh(0.5 * (_dot(xb, wi_ref[0, s]) + bi_ref[0, :, cs]))
        lam = lam_ref[0, :, cs]
        softplus_neg_lam = jnp.maximum(-lam, 0.0) + jnp.log1p(jnp.exp(-jnp.abs(lam)))
        log_a = (tr + 1.0) * (-0.5 * LRU_C * softplus_neg_lam)
        a = jnp.exp(log_a)
        y = -jnp.tanh(log_a) * (a * a + 1.0)
        u = jnp.where(y > 0.0, y * lax.rsqrt(y), 0.0) * ((0.5 * ti + 0.5) * xl)
        for g in range(LRU_SEGS):
            a_ref[s, pl.ds(g * pitch, seg_len), :] = a[g * seg_len:(g + 1) * seg_len]
            u_ref[s, pl.ds(g * pitch, seg_len), :] = u[g * seg_len:(g + 1) * seg_len]

    def body(t, carry):
        hs, ps = carry
        new_h, new_p = [], []
        for s in range(nslab):
            rows = pl.ds(t, LRU_SEGS, stride=pitch)
            av = a_ref[s, rows, :]
            h = av * hs[s] + u_ref[s, rows, :]
            p = av * ps[s]
            h_ref[s, rows, :] = h
            p_ref[s, rows, :] = p
            new_h.append(h)
            new_p.append(p)
        return tuple(new_h), tuple(new_p)

    init = (tuple(zeros8 for _ in range(nslab)), tuple(zeros8 + 1.0 for _ in range(nslab)))
    h_end, p_end = lax.fori_loop(0, seg_len, body, init, unroll=8)

    for s in range(nslab):
        cs = slice(s * V7X_LANES, (s + 1) * V7X_LANES)
        carry = jnp.zeros((1, V7X_LANES), F32)
        for g in range(LRU_SEGS):
            rows = pl.ds(g * pitch, seg_len)
            h = h_ref[s, rows, :] + p_ref[s, rows, :] * carry
            gate = jax.nn.gelu(ly_ref[0, g * seg_len:(g + 1) * seg_len, cs].astype(F32))
            o_ref[0, g * seg_len:(g + 1) * seg_len, cs] = (h * gate).astype(BF16)
            carry = h_end[s][g:g + 1] + p_end[s][g:g + 1] * carry


def _lru(proj, conv_w, conv_b, w_r, b_r, w_i, b_i, lam, l):
    bsz, seq, _ = proj.shape
    w = LRU_COLS
    nslab = w // V7X_LANES
    scan_rows = LRU_SEGS * _lru_pitch(seq // LRU_SEGS)
    vec = lambda: pl.BlockSpec((1, 1, w), lambda b, j: (l, 0, j))
    wsp = lambda: pl.BlockSpec((1, nslab, V7X_LANES, V7X_LANES), lambda b, j: (l, j, 0, 0))
    return pl.pallas_call(
        _lru_kernel,
        out_shape=jax.ShapeDtypeStruct((bsz, seq, D_MODEL), BF16),
        grid=(bsz, D_MODEL // w),
        in_specs=[
            pl.BlockSpec((1, seq, w), lambda b, j: (b, 0, OFF_LX // w + j)),
            pl.BlockSpec((1, seq, w), lambda b, j: (b, 0, OFF_LY // w + j)),
            pl.BlockSpec((1, CONV_W, w), lambda b, j: (l, 0, j)),
            vec(), wsp(), vec(), wsp(), vec(), vec(),
        ],
        out_specs=pl.BlockSpec((1, seq, w), lambda b, j: (b, 0, j)),
        scratch_shapes=[pltpu.VMEM((nslab, seq + V7X_SUBLANES, V7X_LANES), F32)]
                       + [pltpu.VMEM((nslab, scan_rows, V7X_LANES), F32)] * 4,
        compiler_params=_cparams(("parallel", "parallel")),
        name="rg_lru",
    )(proj, proj, conv_w, conv_b, w_r, b_r, w_i, b_i, lam)


def _mprep_kernel(mx_ref, cw_ref, cb_ref, wq_ref, wk_ref, wv_ref, wif_ref, bif_ref, *rest):
    wf_refs, (q_ref, k_ref, v_ref, gate_ref), wb_refs, xs_ref = rest[:4], rest[4:8], rest[8:12], rest[12]
    for wf_ref, wb_ref in zip(wf_refs, wb_refs):
        wb_ref[...] = wf_ref[0].astype(BF16)
    ts = mx_ref.shape[1]

    @pl.when(pl.program_id(1) == 0)
    def _():
        for s in range(xs_ref.shape[0]):
            xs_ref[s, pl.ds(ts, V7X_SUBLANES), :] = jnp.zeros((V7X_SUBLANES, V7X_LANES), F32)

    mxb = mx_ref[0]
    xc = []
    for s in range(xs_ref.shape[0]):
        cs = slice(s * V7X_LANES, (s + 1) * V7X_LANES)
        xs_ref[s, pl.ds(0, V7X_SUBLANES), :] = xs_ref[s, pl.ds(ts, V7X_SUBLANES), :]
        xs_ref[s, pl.ds(V7X_SUBLANES, ts), :] = mxb[:, cs].astype(F32)
        y = _causal_conv(xs_ref.at[s], ts, cw_ref[0, :, cs], cb_ref[0, :, cs])
        xc.append(_silu(y).astype(BF16))
    xc = jnp.concatenate(xc, axis=1)
    acc = jnp.zeros((ts, V7X_LANES), F32) + bif_ref[0]
    nh = MLSTM_HEADS
    for h in range(nh):
        cs = slice(h * MLSTM_DH, (h + 1) * MLSTM_DH)
        mq = _dot(xc[:, cs], wq_ref[0, h]).astype(BF16)
        mk = _dot(xc[:, cs], wk_ref[0, h]).astype(BF16)
        mv = _dot(mxb[:, cs], wv_ref[0, h]).astype(BF16)
        q_ref[0, :, cs] = mq
        k_ref[0, :, cs] = mk
        v_ref[0, :, cs] = mv
        acc = acc + _dot(mq, wif_ref[0, h]) + _dot(mk, wif_ref[0, nh + h]) + _dot(mv, wif_ref[0, 2 * nh + h])
    lane = lax.broadcasted_iota(jnp.int32, acc.shape, 1)
    log_f = jnp.minimum(acc, 0.0) - jnp.log1p(jnp.exp(-jnp.abs(acc)))
    gate_ref[0] = jnp.where(lane >= nh, log_f, acc)


def _mlstm_prep(proj, conv_w, conv_b, wq, wk, wv, wif, bif, merge_ws, l):
    bsz, seq, _ = proj.shape
    width = MLSTM_HEADS * MLSTM_DH
    ts = min(seq, 1024)
    nt = seq // ts
    nh, dh = MLSTM_HEADS, MLSTM_DH
    d = merge_ws[0].shape[-1]
    rows = merge_ws[0].shape[1] // (bsz * nt)
    qkv = jax.ShapeDtypeStruct((bsz, seq, width), BF16)
    wb_shape = jax.ShapeDtypeStruct(merge_ws[0].shape[1:], BF16)
    wspec = lambda: pl.BlockSpec((1, nh, dh, dh), lambda b, t: (l, 0, 0, 0))
    ospec = lambda: pl.BlockSpec((1, ts, width), lambda b, t: (b, t, 0))
    return pl.pallas_call(
        _mprep_kernel,
        out_shape=(qkv, qkv, qkv, jax.ShapeDtypeStruct((bsz, seq, V7X_LANES), F32)) + (wb_shape,) * 4,
        grid=(bsz, nt),
        in_specs=[
            pl.BlockSpec((1, ts, width), lambda b, t: (b, t, OFF_MX // width)),
            pl.BlockSpec((1, CONV_W, width), lambda b, t: (l, 0, 0)),
            pl.BlockSpec((1, 1, width), lambda b, t: (l, 0, 0)),
            wspec(), wspec(), wspec(),
            pl.BlockSpec((1, 3 * nh, dh, V7X_LANES), lambda b, t: (l, 0, 0, 0)),
            pl.BlockSpec((1, 1, V7X_LANES), lambda b, t: (l, 0, 0)),
        ] + [pl.BlockSpec((1, rows, d), lambda b, t: (l, b * nt + t, 0))] * 4,
        out_specs=(ospec(), ospec(), ospec(),
                   pl.BlockSpec((1, ts, V7X_LANES), lambda b, t: (b, t, 0)))
                  + (pl.BlockSpec((rows, d), lambda b, t: (b * nt + t, 0)),) * 4,
        scratch_shapes=[pltpu.VMEM((width // V7X_LANES, ts + V7X_SUBLANES, V7X_LANES), F32)],
        compiler_params=_cparams(("parallel", "arbitrary")),
        name="mlstm_prep",
    )(proj, conv_w, conv_b, wq, wk, wv, wif, bif, *merge_ws)


def _mlstm_kernel(q_ref, k_ref, v_ref, gate_ref, mo_ref, mn_ref, o_ref, c_ref, n_ref, m_ref):
    head = pl.program_id(1)

    @pl.when(pl.program_id(2) == 0)
    def _():
        c_ref[...] = jnp.zeros_like(c_ref)
        n_ref[...] = jnp.zeros_like(n_ref)
        m_ref[...] = jnp.zeros_like(m_ref)

    gates = gate_ref[0]
    ts = gates.shape[0]
    nc = ts // CHUNK
    lane = lax.broadcasted_iota(jnp.int32, gates.shape, 1)
    ic = jnp.sum(jnp.where(lane == head, gates, 0.0), axis=1, keepdims=True).reshape(nc, CHUNK, 1)
    lf = jnp.sum(jnp.where(lane == head + MLSTM_HEADS, gates, 0.0), axis=1, keepdims=True).reshape(nc, CHUNK, 1)

    ii = lax.broadcasted_iota(jnp.int32, (nc, CHUNK, CHUNK), 1)
    jj = lax.broadcasted_iota(jnp.int32, (nc, CHUNK, CHUNK), 2)
    causal = ii >= jj
    diag = ii == jj
    b_row = jnp.sum(jnp.where(ii <= jj, lf, 0.0), axis=1, keepdims=True)
    lf_row = jnp.sum(jnp.where(diag, lf, 0.0), axis=1, keepdims=True)
    ic_row = jnp.sum(jnp.where(diag, ic, 0.0), axis=1, keepdims=True)
    b_col = jnp.sum(jnp.where(causal, lf_row, 0.0), axis=2, keepdims=True)
    dmat = jnp.where(causal, b_col - b_row + ic_row, -jnp.inf)
    row_max = jnp.max(dmat, axis=2, keepdims=True)

    m_s = m_ref[...]
    m_t_list, m_prev_list = [], []
    for c in range(nc):
        m_prev_list.append(m_s)
        m_tc = jnp.maximum(b_col[c] + m_s, row_max[c])
        m_t_list.append(m_tc)
        m_s = m_tc[CHUNK - 1:]
    m_ref[...] = m_s

    q = q_ref[0]
    v = v_ref[0]
    kb = k_ref[0] * jnp.asarray(MLSTM_DH ** -0.5, BF16)
    m_t = jnp.stack(m_t_list)
    m_prev = jnp.stack(m_prev_list)
    w_inter = jnp.exp(b_col + m_prev - m_t)
    b_last = b_col[:, CHUNK - 1:, :]
    m_new = m_t[:, CHUNK - 1:, :]
    w_k = jnp.exp(b_last - b_col + ic - m_new)
    decay = jnp.exp(b_last + m_prev - m_new)
    p = jnp.exp(dmat - m_t)
    kwb = kb.reshape(nc, CHUNK, MLSTM_DH) * w_k.astype(BF16)
    ones = jnp.ones((V7X_SUBLANES, CHUNK), BF16)

    chunks = [slice(c * CHUNK, (c + 1) * CHUNK) for c in range(nc)]
    qk = [_dot_nt(q[sl], kb[sl]) for sl in chunks]
    kv = [_dot_tn(kwb[c], v[sl]) for c, sl in enumerate(chunks)]
    n_add = [_dot(ones, kwb[c]) for c in range(nc)]
    s = [qk[c] * p[c] for c in range(nc)]
    intra = [_dot(s[c].astype(BF16), v[sl]) for c, sl in enumerate(chunks)]
    c_s = c_ref[...]
    n_s = n_ref[...]
    c_states, n_states = [], []
    for c in range(nc):
        c_states.append(c_s.astype(BF16))
        n_states.append(n_s)
        c_s = decay[c] * c_s + kv[c]
        n_s = decay[c] * n_s + n_add[c][:1]
    c_ref[...] = c_s
    n_ref[...] = n_s
    inter = [_dot(q[sl], c_states[c]) for c, sl in enumerate(chunks)]
    w_inter = w_inter.reshape(ts, 1)
    num = jnp.concatenate(intra, axis=0) + w_inter * jnp.concatenate(inter, axis=0)
    s_sum = jnp.sum(jnp.stack(s), axis=2, keepdims=True).reshape(ts, 1)
    qn = [_dot_nt(q[sl], jnp.broadcast_to(n_states[c], (V7X_SUBLANES, MLSTM_DH)).astype(BF16))[:, :1]
          for c, sl in enumerate(chunks)]
    den = s_sum + w_inter * jnp.concatenate(qn, axis=0)
    h = num / jnp.maximum(jnp.abs(den), jnp.exp(-m_t.reshape(ts, 1)))
    o = _sigmoid(mo_ref[0].astype(F32)) * h
    o = o * lax.rsqrt(jnp.mean(o * o, axis=-1, keepdims=True) + EPS)
    o_ref[0] = (o * mn_ref[0]).astype(BF16)


def _mlstm(mq, mk, mv, gates, proj, m_norm, l):
    bsz, seq, width = mq.shape
    dh = MLSTM_DH
    ts = min(seq, 2048)
    hspec = lambda: pl.BlockSpec((1, ts, dh), lambda b, h, t: (b, t, h))
    return pl.pallas_call(
        _mlstm_kernel,
        out_shape=jax.ShapeDtypeStruct((bsz, seq, width), BF16),
        grid=(bsz, MLSTM_HEADS, seq // ts),
        in_specs=[
            hspec(), hspec(), hspec(),
            pl.BlockSpec((1, ts, V7X_LANES), lambda b, h, t: (b, t, 0)),
            pl.BlockSpec((1, ts, dh), lambda b, h, t: (b, t, OFF_MO // dh + h)),
            pl.BlockSpec((1, 1, dh), lambda b, h, t: (l, 0, h)),
        ],
        out_specs=hspec(),
        scratch_shapes=[pltpu.VMEM((dh, dh), F32), pltpu.VMEM((1, dh), F32), pltpu.VMEM((1, 1), F32)],
        compiler_params=_cparams(("parallel", "parallel", "arbitrary")),
        name="mlstm",
    )(mq, mk, mv, gates, proj, m_norm)


def _merge_kernel(x_ref, ret_ref, lru_ref, mls_ref, g0_ref, g1_ref, g2_ref, mod_ref,
                  wr_ref, wl_ref, wm_ref, wo_ref, w1f_ref, w2f_ref, o_ref, w1b_ref, w2b_ref):
    w1b_ref[...] = w1f_ref[0].astype(BF16)
    w2b_ref[...] = w2f_ref[0].astype(BF16)
    merged = (_sigmoid(g0_ref[0].astype(F32)) * _dot(ret_ref[0], wr_ref[...])
              + _sigmoid(g1_ref[0].astype(F32)) * _dot(lru_ref[0], wl_ref[...])
              + _sigmoid(g2_ref[0].astype(F32)) * _dot(mls_ref[0], wm_ref[...]))
    y = _dot(merged.astype(BF16), wo_ref[...])
    o_ref[0] = x_ref[0] + mod_ref[0, 0, 2:3, :] * y


def _merge(x, ret, lru, mls, proj, mod, w_br_ret, w_br_lru, w_br_mlstm, w_out, w_ff1, w_ff2, l):
    bsz, seq, d = x.shape
    dff = w_ff1.shape[-1]
    tm = min(seq, 512)
    nt = seq // tm
    steps = bsz * nt
    r1, r2 = d // steps, dff // steps
    gb = OFF_GATE // d
    tok = lambda: pl.BlockSpec((1, tm, d), lambda b, i: (b, i, 0))
    wsp = lambda: pl.BlockSpec((d, d), lambda b, i: (0, 0))
    return pl.pallas_call(
        _merge_kernel,
        out_shape=(jax.ShapeDtypeStruct((bsz, seq, d), F32),
                   jax.ShapeDtypeStruct((d, dff), BF16), jax.ShapeDtypeStruct((dff, d), BF16)),
        grid=(bsz, nt),
        in_specs=[
            tok(), tok(), tok(), tok(),
            pl.BlockSpec((1, tm, d), lambda b, i: (b, i, gb)),
            pl.BlockSpec((1, tm, d), lambda b, i: (b, i, gb + 1)),
            pl.BlockSpec((1, tm, d), lambda b, i: (b, i, gb + 2)),
            pl.BlockSpec((1, 1, 6, d), lambda b, i: (l, b, 0, 0)),
            wsp(), wsp(), wsp(), wsp(),
            pl.BlockSpec((1, r1, dff), lambda b, i: (l, b * nt + i, 0)),
            pl.BlockSpec((1, r2, d), lambda b, i: (l, b * nt + i, 0)),
        ],
        out_specs=(tok(),
                   pl.BlockSpec((r1, dff), lambda b, i: (b * nt + i, 0)),
                   pl.BlockSpec((r2, d), lambda b, i: (b * nt + i, 0))),
        compiler_params=_cparams(("parallel", "parallel")),
        name="merge_out",
    )(x, ret, lru, mls, proj, proj, proj, mod, w_br_ret, w_br_lru, w_br_mlstm, w_out, w_ff1, w_ff2)


def _ffn_kernel(x_ref, mod_ref, gain_ref, w1_ref, w2_ref, *rest, last):
    x = x_ref[0]
    h = _modulated_norm(x, gain_ref[0], mod_ref[0, 0, 4:5, :], mod_ref[0, 0, 3:4, :])
    a = jnp.square(jnp.maximum(_dot(h.astype(BF16), w1_ref[...]), 0.0))
    y = x + mod_ref[0, 0, 5:6, :] * _dot(a.astype(BF16), w2_ref[...])
    if last:
        fgain_ref, o_ref = rest
        o_ref[0] = y * lax.rsqrt(jnp.mean(y * y, axis=-1, keepdims=True) + EPS) * fgain_ref[...]
    else:
        nmod_ref, ngain_ref, o_ref, hn_ref = rest
        o_ref[0] = y
        hn = _modulated_norm(y, ngain_ref[0], nmod_ref[0, 0, 1:2, :], nmod_ref[0, 0, 0:1, :])
        hn_ref[0] = hn.astype(BF16)


def _ffn(x, mod, gain, w1, w2, mix_gain, final_gain, l, last):
    bsz, seq, d = x.shape
    dff = w1.shape[-1]
    tm = min(seq, 512)
    tok = lambda: pl.BlockSpec((1, tm, d), lambda b, i: (b, i, 0))
    in_specs = [
        tok(),
        pl.BlockSpec((1, 1, 6, d), lambda b, i: (l, b, 0, 0)),
        pl.BlockSpec((1, 1, d), lambda b, i: (l, 0, 0)),
        pl.BlockSpec((d, dff), lambda b, i: (0, 0), pipeline_mode=pl.Buffered(1)),
        pl.BlockSpec((dff, d), lambda b, i: (0, 0), pipeline_mode=pl.Buffered(1)),
    ]
    if last:
        in_specs.append(pl.BlockSpec((1, d), lambda b, i: (0, 0)))
        extra = (final_gain,)
        out_shape = jax.ShapeDtypeStruct((bsz, seq, d), F32)
        out_specs = tok()
    else:
        in_specs += [pl.BlockSpec((1, 1, 6, d), lambda b, i: (l + 1, b, 0, 0)),
                     pl.BlockSpec((1, 1, d), lambda b, i: (l + 1, 0, 0))]
        extra = (mod, mix_gain)
        out_shape = (jax.ShapeDtypeStruct((bsz, seq, d), F32), jax.ShapeDtypeStruct((bsz, seq, d), BF16))
        out_specs = (tok(), tok())
    return pl.pallas_call(
        functools.partial(_ffn_kernel, last=last),
        out_shape=out_shape,
        grid=(bsz, seq // tm),
        in_specs=in_specs,
        out_specs=out_specs,
        compiler_params=_cparams(("parallel", "parallel")),
        name="ffn",
    )(x, mod, gain, w1, w2, *extra)


def _block_diag_tiles(w, tile):
    depth, nb, bs, _ = w.shape
    rows = nb * bs
    sel = jnp.tile(jnp.eye(bs, dtype=w.dtype), (1, tile // bs))
    dense = jnp.einsum('lre,ec->lrc', w.reshape(depth, rows, bs), sel, precision=lax.Precision.HIGHEST)
    r = (np.arange(rows) % tile) // bs
    c = np.arange(tile) // bs
    dense = jnp.where(jnp.asarray(r[:, None] == c[None, :]), dense, 0.0)
    return dense.reshape(depth, rows // tile, tile, tile)


def kernel(x, c, positions, w_ada, b_ada, norm_mix, norm_mlp, w_in, lru_conv_w, lru_conv_b, lru_w_r, lru_b_r, lru_w_i, lru_b_i, lru_lambda, m_conv_w, m_conv_b, m_w_q, m_w_k, m_w_v, m_w_if, m_b_if, m_norm, w_br_ret, w_br_lru, w_br_mlstm, w_out, w_ff1, w_ff2, final_norm):
    depth = w_in.shape[0]
    bsz, seq, d = x.shape
    nh, dh = MLSTM_HEADS, MLSTM_DH

    mod = _ada(c, w_ada, b_ada).reshape(depth, bsz, 6, d)
    cc, ss = _rope_tables(positions)

    vec = lambda a: a.reshape(depth, 1, a.shape[-1])
    w_r_t = _block_diag_tiles(lru_w_r, V7X_LANES).astype(BF16)
    w_i_t = _block_diag_tiles(lru_w_i, V7X_LANES).astype(BF16)
    wq_t = _block_diag_tiles(m_w_q, dh).astype(BF16)
    wk_t = _block_diag_tiles(m_w_k, dh).astype(BF16)
    wv_t = _block_diag_tiles(m_w_v, dh).astype(BF16)
    wif_t = jnp.pad(m_w_if, ((0, 0), (0, 0), (0, V7X_LANES - 2 * nh))).reshape(depth, 3 * nh, dh, V7X_LANES).astype(BF16)
    bif_t = jnp.pad(m_b_if, ((0, 0), (0, V7X_LANES - 2 * nh))).reshape(depth, 1, V7X_LANES)
    g_mix, g_mlp = vec(norm_mix), vec(norm_mlp)
    l_cb, l_br, l_bi, l_lam = vec(lru_conv_b), vec(lru_b_r), vec(lru_b_i), vec(lru_lambda)
    m_cb, m_nrm = vec(m_conv_b), vec(m_norm)
    f_gain = final_norm.reshape(1, d)

    h = _prenorm(x, mod, g_mix, 0)
    for l in range(depth):
        proj = _inproj(h, w_in, l)
        ret = _retention(proj, cc, ss)
        lru = _lru(proj, lru_conv_w, l_cb, w_r_t, l_br, w_i_t, l_bi, l_lam, l)
        mq, mk, mv, gates, w_br_ret_b, w_br_lru_b, w_br_mls_b, w_out_b = _mlstm_prep(
            proj, m_conv_w, m_cb, wq_t, wk_t, wv_t, wif_t, bif_t, (w_br_ret, w_br_lru, w_br_mlstm, w_out), l)
        mls = _mlstm(mq, mk, mv, gates, proj, m_nrm, l)
        x, w_ff1_b, w_ff2_b = _merge(x, ret, lru, mls, proj, mod, w_br_ret_b, w_br_lru_b, w_br_mls_b, w_out_b,
                                     w_ff1, w_ff2, l)
        if l == depth - 1:
            return _ffn(x, mod, g_mlp, w_ff1_b, w_ff2_b, g_mix, f_gain, l, last=True)
        x, h = _ffn(x, mod, g_mlp, w_ff1_b, w_ff2_b, g_mix, f_gain, l, last=False)
```

```python
import functools

import jax
import jax.numpy as jnp
import numpy as np
from jax import lax
from jax.experimental import pallas as pl
from jax.experimental.pallas import tpu as pltpu

F32 = jnp.float32
BF16 = jnp.bfloat16

D_MODEL = 1024
RET_HEADS = 4
RET_DK = 128
RET_DV = 256
ROPE_BASE = 10000.0
LRU_C = 8.0
CONV_W = 4
MLSTM_HEADS = 4
MLSTM_DH = 256
CHUNK = 128
EPS = 1e-6

OFF_RQ, OFF_RK, OFF_RV, OFF_RG = 0, 512, 1024, 2048
OFF_LX, OFF_LY, OFF_MX, OFF_MO, OFF_GATE = 3072, 4096, 5120, 6144, 7168

V7X_LANES = 128
V7X_SUBLANES = 8
V7X_VMEM_LIMIT_BYTES = 56 * 1024 * 1024


def _cparams(sem):
    return pltpu.CompilerParams(dimension_semantics=sem, vmem_limit_bytes=V7X_VMEM_LIMIT_BYTES)


def _dot(a, b):
    return jnp.dot(a, b, preferred_element_type=F32)


def _dot_nt(a, b):
    return lax.dot_general(a, b, (((1,), (1,)), ((), ())), preferred_element_type=F32)


def _dot_tn(a, b):
    return lax.dot_general(a, b, (((0,), (0,)), ((), ())), preferred_element_type=F32)


def _sigmoid(x):
    return 0.5 * jnp.tanh(0.5 * x) + 0.5


def _silu(x):
    return x * _sigmoid(x)


def _ada_kernel(c_ref, w_ref, b_ref, o_ref):
    cond = _silu(c_ref[...])
    o_ref[0] = _dot(cond.astype(BF16), w_ref[0].astype(BF16)) + b_ref[0]


def _ada(c, w_ada, b_ada):
    depth, d, n = w_ada.shape
    bsz = c.shape[0]
    tn = 2048
    return pl.pallas_call(
        _ada_kernel,
        out_shape=jax.ShapeDtypeStruct((depth, bsz, n), F32),
        grid=(depth, n // tn),
        in_specs=[
            pl.BlockSpec((bsz, d), lambda l, j: (0, 0)),
            pl.BlockSpec((1, d, tn), lambda l, j: (l, 0, j)),
            pl.BlockSpec((1, 1, tn), lambda l, j: (l, 0, j)),
        ],
        out_specs=pl.BlockSpec((1, bsz, tn), lambda l, j: (l, 0, j)),
        compiler_params=_cparams(("parallel", "parallel")),
        name="ada_mod",
    )(c, w_ada, b_ada.reshape(depth, 1, n))


def _rope_kernel(pos_ref, invf_ref, cc_ref, ss_ref):
    half = RET_DK // 2
    rows = pos_ref.shape[1] // 2
    lo = lax.broadcasted_iota(jnp.int32, (rows, RET_DK), 1) < half
    ang = jnp.where(lo, pos_ref[0, :rows], pos_ref[0, rows:]) * invf_ref[...]
    c, s = jnp.cos(ang), jnp.sin(ang)
    c_sw, s_sw = pltpu.roll(c, half, 1), pltpu.roll(s, half, 1)
    cc_ref[0, :rows] = jnp.where(lo, c, c_sw)
    cc_ref[0, rows:] = jnp.where(lo, c_sw, c)
    ss_ref[0, :rows] = jnp.where(lo, -s, s_sw)
    ss_ref[0, rows:] = jnp.where(lo, -s_sw, s)


def _rope_tables(positions):
    bsz, seq = positions.shape
    half = RET_DK // 2
    inv_freq = ROPE_BASE ** (-jnp.arange(half, dtype=F32) / half)
    invf = jnp.concatenate([inv_freq, inv_freq]).reshape(1, RET_DK)
    pos = positions.astype(F32).reshape(bsz, seq, 1)
    ts = min(seq, 512)
    out = jax.ShapeDtypeStruct((bsz, seq, RET_DK), F32)
    return pl.pallas_call(
        _rope_kernel,
        out_shape=(out, out),
        grid=(bsz, seq // ts),
        in_specs=[
            pl.BlockSpec((1, ts, 1), lambda b, t: (b, t, 0)),
            pl.BlockSpec((1, RET_DK), lambda b, t: (0, 0)),
        ],
        out_specs=(pl.BlockSpec((1, ts, RET_DK), lambda b, t: (b, t, 0)),
                   pl.BlockSpec((1, ts, RET_DK), lambda b, t: (b, t, 0))),
        compiler_params=_cparams(("parallel", "parallel")),
        name="rope_tables",
    )(pos, invf)


def _modulated_norm(x, gain, scale, shift):
    xn = x * lax.rsqrt(jnp.mean(x * x, axis=-1, keepdims=True) + EPS)
    return xn * gain * (1.0 + scale) + shift


def _prenorm_kernel(x_ref, mod_ref, gain_ref, o_ref):
    h = _modulated_norm(x_ref[0], gain_ref[0], mod_ref[0, 0, 1:2, :], mod_ref[0, 0, 0:1, :])
    o_ref[0] = h.astype(BF16)


def _prenorm(x, mod, gain, l):
    bsz, seq, d = x.shape
    tm = min(seq, 2048)
    tok = lambda: pl.BlockSpec((1, tm, d), lambda b, i: (b, i, 0))
    return pl.pallas_call(
        _prenorm_kernel,
        out_shape=jax.ShapeDtypeStruct((bsz, seq, d), BF16),
        grid=(bsz, seq // tm),
        in_specs=[
            tok(),
            pl.BlockSpec((1, 1, 6, d), lambda b, i: (l, b, 0, 0)),
            pl.BlockSpec((1, 1, d), lambda b, i: (l, 0, 0)),
        ],
        out_specs=tok(),
        compiler_params=_cparams(("parallel", "parallel")),
        name="prenorm",
    )(x, mod, gain)


INPROJ_N_SPLIT = 2


def _inproj_kernel(h_ref, w_ref, o_ref, wb_ref):
    @pl.when((pl.program_id(1) == 0) & (pl.program_id(2) == 0))
    def _():
        wb_ref[...] = w_ref[0].astype(BF16)

    part = wb_ref.shape[1] // INPROJ_N_SPLIT
    for n in range(INPROJ_N_SPLIT):
        cs = slice(n * part, (n + 1) * part)
        o_ref[0, :, cs] = _dot(h_ref[0], wb_ref[:, cs]).astype(BF16)


def _inproj(h, w_in, l):
    bsz, seq, d = h.shape
    n = w_in.shape[-1]
    tm = min(seq, 2048)
    tn = 2048
    return pl.pallas_call(
        _inproj_kernel,
        out_shape=jax.ShapeDtypeStruct((bsz, seq, n), BF16),
        grid=(n // tn, bsz, seq // tm),
        in_specs=[
            pl.BlockSpec((1, tm, d), lambda j, b, i: (b, i, 0)),
            pl.BlockSpec((1, d, tn), lambda j, b, i: (l, 0, j)),
        ],
        out_specs=pl.BlockSpec((1, tm, tn), lambda j, b, i: (b, i, j)),
        scratch_shapes=[pltpu.VMEM((d, tn), BF16)],
        compiler_params=_cparams(("arbitrary", "arbitrary", "arbitrary")),
        name="in_proj",
    )(h, w_in)


def _ret_kernel(q_ref, k_ref, v_ref, g_ref, cc_ref, ss_ref, o_ref, state_ref):
    head = pl.program_id(1)

    @pl.when(pl.program_id(2) == 0)
    def _():
        state_ref[...] = jnp.zeros_like(state_ref)

    hv = jnp.zeros((1, 1), F32) + head.astype(F32)
    log_gamma = jnp.log1p(-jnp.exp2(-5.0 - hv))
    ii = lax.broadcasted_iota(jnp.int32, (CHUNK, CHUNK), 0)
    jj = lax.broadcasted_iota(jnp.int32, (CHUNK, CHUNK), 1)
    causal = ii >= jj
    diff = jnp.where(causal, (ii - jj).astype(F32), 0.0)
    decay_intra = jnp.where(causal, jnp.exp(log_gamma * diff), 0.0)
    pos = lax.broadcasted_iota(jnp.int32, (CHUNK, 1), 0).astype(F32)
    decay_q = jnp.exp(log_gamma * (pos + 1.0))
    decay_k = jnp.exp(log_gamma * (CHUNK - 1.0 - pos))
    decay_chunk = jnp.exp(log_gamma * CHUNK)

    cc = cc_ref[0]
    ss = ss_ref[0]
    q = q_ref[0].astype(F32)
    k = k_ref[0].astype(F32)
    q = q * cc + pltpu.roll(q, RET_DK // 2, 1) * ss
    k = (k * cc + pltpu.roll(k, RET_DK // 2, 1) * ss) * RET_DK ** -0.5

    ts = q.shape[0]
    nc = ts // CHUNK
    qb = q.astype(BF16)
    kb = k.astype(BF16)
    qd = (q.reshape(nc, CHUNK, RET_DK) * decay_q).astype(BF16)
    kd = (k.reshape(nc, CHUNK, RET_DK) * decay_k).astype(BF16)
    v = v_ref[0]
    chunks = [slice(c * CHUNK, (c + 1) * CHUNK) for c in range(nc)]
    scores = [_dot_nt(qb[sl], kb[sl]) for sl in chunks]
    kv = [_dot_tn(kd[c], v[sl]) for c, sl in enumerate(chunks)]
    scores = [(s * decay_intra).astype(BF16) for s in scores]
    intra = [_dot(scores[c], v[sl]) for c, sl in enumerate(chunks)]
    state = state_ref[...]
    states = []
    for c in range(nc):
        states.append(state.astype(BF16))
        state = decay_chunk * state + kv[c]
    state_ref[...] = state
    inter = [_dot(qd[c], states[c]) for c in range(nc)]
    out = jnp.concatenate(intra, axis=0) + jnp.concatenate(inter, axis=0)
    out = out * lax.rsqrt(jnp.mean(out * out, axis=-1, keepdims=True) + EPS)
    o_ref[0] = (out * _silu(g_ref[0].astype(F32))).astype(BF16)


def _retention(proj, cc, ss):
    bsz, seq, _ = proj.shape
    ts = min(seq, 2048)
    qb, kb = OFF_RQ // RET_DK, OFF_RK // RET_DK
    vb, gb = OFF_RV // RET_DV, OFF_RG // RET_DV
    return pl.pallas_call(
        _ret_kernel,
        out_shape=jax.ShapeDtypeStruct((bsz, seq, RET_HEADS * RET_DV), BF16),
        grid=(bsz, RET_HEADS, seq // ts),
        in_specs=[
            pl.BlockSpec((1, ts, RET_DK), lambda b, h, t: (b, t, qb + h)),
            pl.BlockSpec((1, ts, RET_DK), lambda b, h, t: (b, t, kb + h)),
            pl.BlockSpec((1, ts, RET_DV), lambda b, h, t: (b, t, vb + h)),
            pl.BlockSpec((1, ts, RET_DV), lambda b, h, t: (b, t, gb + h)),
            pl.BlockSpec((1, ts, RET_DK), lambda b, h, t: (b, t, 0)),
            pl.BlockSpec((1, ts, RET_DK), lambda b, h, t: (b, t, 0)),
        ],
        out_specs=pl.BlockSpec((1, ts, RET_DV), lambda b, h, t: (b, t, h)),
        scratch_shapes=[pltpu.VMEM((RET_DK, RET_DV), F32)],
        compiler_params=_cparams(("parallel", "parallel", "arbitrary")),
        name="retention",
    )(proj, proj, proj, proj, cc, ss)


def _causal_conv(xs_ref, rows, w, b):
    y = b
    for k in range(CONV_W):
        s = CONV_W - 1 - k
        y = y + w[k:k + 1, :] * xs_ref[pl.ds(V7X_SUBLANES - s, rows), :]
    return y


LRU_COLS = 512
LRU_SEGS = V7X_SUBLANES


def _lru_pitch(seg_len):
    assert seg_len % V7X_SUBLANES == 0
    return seg_len + 4


def _lru_kernel(lx_ref, ly_ref, cw_ref, cb_ref, wr_ref, br_ref, wi_ref, bi_ref, lam_ref, o_ref,
                xs_ref, a_ref, u_ref, h_ref, p_ref):
    seq = lx_ref.shape[1]
    nslab = lx_ref.shape[2] // V7X_LANES
    seg_len = seq // LRU_SEGS
    pitch = _lru_pitch(seg_len)
    zeros8 = jnp.zeros((V7X_SUBLANES, V7X_LANES), F32)

    for s in range(nslab):
        cs = slice(s * V7X_LANES, (s + 1) * V7X_LANES)
        xs_ref[s, pl.ds(0, V7X_SUBLANES), :] = zeros8
        xs_ref[s, pl.ds(V7X_SUBLANES, seq), :] = lx_ref[0, :, cs].astype(F32)
        xl = _causal_conv(xs_ref.at[s], seq, cw_ref[0, :, cs], cb_ref[0, :, cs])
        xb = xl.astype(BF16)
        tr = jnp.tanh(_dot(xb, wr_ref[0, s]) + br_ref[0, :, cs])
        ti = jnp.tanh(_dot(xb, wi_ref[0, s]) + bi_ref[0, :, cs])
        lam = lam_ref[0, :, cs]
        softplus_neg_lam = jnp.maximum(-lam, 0.0) + jnp.log1p(jnp.exp(-jnp.abs(lam)))
        log_a = (tr + 1.0) * (-0.5 * LRU_C * softplus_neg_lam)
        a = jnp.exp(log_a)
        y = -jnp.tanh(log_a) * (a * a + 1.0)
        u = jnp.where(y > 0.0, y * lax.rsqrt(y), 0.0) * ((0.5 * ti + 0.5) * xl)
        for g in range(LRU_SEGS):
            a_ref[s, pl.ds(g * pitch, seg_len), :] = a[g * seg_len:(g + 1) * seg_len]
            u_ref[s, pl.ds(g * pitch, seg_len), :] = u[g * seg_len:(g + 1) * seg_len]

    def body(t, carry):
        hs, ps = carry
        new_h, new_p = [], []
        for s in range(nslab):
            rows = pl.ds(t, LRU_SEGS, stride=pitch)
            av = a_ref[s, rows, :]
            h = av * hs[s] + u_ref[s, rows, :]
            p = av * ps[s]
            h_ref[s, rows, :] = h
            p_ref[s, rows, :] = p
            new_h.append(h)
            new_p.append(p)
        return tuple(new_h), tuple(new_p)

    init = (tuple(zeros8 for _ in range(nslab)), tuple(zeros8 + 1.0 for _ in range(nslab)))
    h_end, p_end = lax.fori_loop(0, seg_len, body, init, unroll=8)

    for s in range(nslab):
        cs = slice(s * V7X_LANES, (s + 1) * V7X_LANES)
        carry = jnp.zeros((1, V7X_LANES), F32)
        for g in range(LRU_SEGS):
            rows = pl.ds(g * pitch, seg_len)
            h = h_ref[s, rows, :] + p_ref[s, rows, :] * carry
            gate = jax.nn.gelu(ly_ref[0, g * seg_len:(g + 1) * seg_len, cs].astype(F32))
            o_ref[0, g * seg_len:(g + 1) * seg_len, cs] = (h * gate).astype(BF16)
            carry = h_end[s][g:g + 1] + p_end[s][g:g + 1] * carry


def _lru(proj, conv_w, conv_b, w_r, b_r, w_i, b_i, lam, l):
    bsz, seq, _ = proj.shape
    w = LRU_COLS
    nslab = w // V7X_LANES
    scan_rows = LRU_SEGS * _lru_pitch(seq // LRU_SEGS)
    vec = lambda: pl.BlockSpec((1, 1, w), lambda b, j: (l, 0, j))
    wsp = lambda: pl.BlockSpec((1, nslab, V7X_LANES, V7X_LANES), lambda b, j: (l, j, 0, 0))
    return pl.pallas_call(
        _lru_kernel,
        out_shape=jax.ShapeDtypeStruct((bsz, seq, D_MODEL), BF16),
        grid=(bsz, D_MODEL // w),
        in_specs=[
            pl.BlockSpec((1, seq, w), lambda b, j: (b, 0, OFF_LX // w + j)),
            pl.BlockSpec((1, seq, w), lambda b, j: (b, 0, OFF_LY // w + j)),
            pl.BlockSpec((1, CONV_W, w), lambda b, j: (l, 0, j)),
            vec(), wsp(), vec(), wsp(), vec(), vec(),
        ],
        out_specs=pl.BlockSpec((1, seq, w), lambda b, j: (b, 0, j)),
        scratch_shapes=[pltpu.VMEM((nslab, seq + V7X_SUBLANES, V7X_LANES), F32)]
                       + [pltpu.VMEM((nslab, scan_rows, V7X_LANES), F32)] * 4,
        compiler_params=_cparams(("parallel", "parallel")),
        name="rg_lru",
    )(proj, proj, conv_w, conv_b, w_r, b_r, w_i, b_i, lam)


def _mprep_kernel(mx_ref, cw_ref, cb_ref, wq_ref, wk_ref, wv_ref, wif_ref, bif_ref, *rest):
    wf_refs, (q_ref, k_ref, v_ref, gate_ref), wb_refs, xs_ref = rest[:4], rest[4:8], rest[8:12], rest[12]
    for wf_ref, wb_ref in zip(wf_refs, wb_refs):
        wb_ref[...] = wf_ref[0].astype(BF16)
    ts = mx_ref.shape[1]

    @pl.when(pl.program_id(1) == 0)
    def _():
        for s in range(xs_ref.shape[0]):
            xs_ref[s, pl.ds(ts, V7X_SUBLANES), :] = jnp.zeros((V7X_SUBLANES, V7X_LANES), F32)

    mxb = mx_ref[0]
    xc = []
    for s in range(xs_ref.shape[0]):
        cs = slice(s * V7X_LANES, (s + 1) * V7X_LANES)
        xs_ref[s, pl.ds(0, V7X_SUBLANES), :] = xs_ref[s, pl.ds(ts, V7X_SUBLANES), :]
        xs_ref[s, pl.ds(V7X_SUBLANES, ts), :] = mxb[:, cs].astype(F32)
        y = _causal_conv(xs_ref.at[s], ts, cw_ref[0, :, cs], cb_ref[0, :, cs])
        xc.append(_silu(y).astype(BF16))
    xc = jnp.concatenate(xc, axis=1)
    acc = jnp.zeros((ts, V7X_LANES), F32) + bif_ref[0]
    nh = MLSTM_HEADS
    for h in range(nh):
        cs = slice(h * MLSTM_DH, (h + 1) * MLSTM_DH)
        mq = _dot(xc[:, cs], wq_ref[0, h]).astype(BF16)
        mk = _dot(xc[:, cs], wk_ref[0, h]).astype(BF16)
        mv = _dot(mxb[:, cs], wv_ref[0, h]).astype(BF16)
        q_ref[0, :, cs] = mq
        k_ref[0, :, cs] = mk
        v_ref[0, :, cs] = mv
        acc = acc + _dot(mq, wif_ref[0, h]) + _dot(mk, wif_ref[0, nh + h]) + _dot(mv, wif_ref[0, 2 * nh + h])
    lane = lax.broadcasted_iota(jnp.int32, acc.shape, 1)
    log_f = jnp.minimum(acc, 0.0) - jnp.log1p(jnp.exp(-jnp.abs(acc)))
    gate_ref[0] = jnp.where(lane >= nh, log_f, acc)


def _mlstm_prep(proj, conv_w, conv_b, wq, wk, wv, wif, bif, merge_ws, l):
    bsz, seq, _ = proj.shape
    width = MLSTM_HEADS * MLSTM_DH
    ts = min(seq, 1024)
    nt = seq // ts
    nh, dh = MLSTM_HEADS, MLSTM_DH
    d = merge_ws[0].shape[-1]
    rows = merge_ws[0].shape[1] // (bsz * nt)
    qkv = jax.ShapeDtypeStruct((bsz, seq, width), BF16)
    wb_shape = jax.ShapeDtypeStruct(merge_ws[0].shape[1:], BF16)
    wspec = lambda: pl.BlockSpec((1, nh, dh, dh), lambda b, t: (l, 0, 0, 0))
    ospec = lambda: pl.BlockSpec((1, ts, width), lambda b, t: (b, t, 0))
    return pl.pallas_call(
        _mprep_kernel,
        out_shape=(qkv, qkv, qkv, jax.ShapeDtypeStruct((bsz, seq, V7X_LANES), F32)) + (wb_shape,) * 4,
        grid=(bsz, nt),
        in_specs=[
            pl.BlockSpec((1, ts, width), lambda b, t: (b, t, OFF_MX // width)),
            pl.BlockSpec((1, CONV_W, width), lambda b, t: (l, 0, 0)),
            pl.BlockSpec((1, 1, width), lambda b, t: (l, 0, 0)),
            wspec(), wspec(), wspec(),
            pl.BlockSpec((1, 3 * nh, dh, V7X_LANES), lambda b, t: (l, 0, 0, 0)),
            pl.BlockSpec((1, 1, V7X_LANES), lambda b, t: (l, 0, 0)),
        ] + [pl.BlockSpec((1, rows, d), lambda b, t: (l, b * nt + t, 0))] * 4,
        out_specs=(ospec(), ospec(), ospec(),
                   pl.BlockSpec((1, ts, V7X_LANES), lambda b, t: (b, t, 0)))
                  + (pl.BlockSpec((rows, d), lambda b, t: (b * nt + t, 0)),) * 4,
        scratch_shapes=[pltpu.VMEM((width // V7X_LANES, ts + V7X_SUBLANES, V7X_LANES), F32)],
        compiler_params=_cparams(("parallel", "arbitrary")),
        name="mlstm_prep",
    )(proj, conv_w, conv_b, wq, wk, wv, wif, bif, *merge_ws)


def _mlstm_kernel(q_ref, k_ref, v_ref, gate_ref, mo_ref, mn_ref, o_ref, c_ref, n_ref, m_ref):
    head = pl.program_id(1)

    @pl.when(pl.program_id(2) == 0)
    def _():
        c_ref[...] = jnp.zeros_like(c_ref)
        n_ref[...] = jnp.zeros_like(n_ref)
        m_ref[...] = jnp.zeros_like(m_ref)

    gates = gate_ref[0]
    ts = gates.shape[0]
    nc = ts // CHUNK
    lane = lax.broadcasted_iota(jnp.int32, gates.shape, 1)
    ic = jnp.sum(jnp.where(lane == head, gates, 0.0), axis=1, keepdims=True).reshape(nc, CHUNK, 1)
    lf = jnp.sum(jnp.where(lane == head + MLSTM_HEADS, gates, 0.0), axis=1, keepdims=True).reshape(nc, CHUNK, 1)

    ii = lax.broadcasted_iota(jnp.int32, (nc, CHUNK, CHUNK), 1)
    jj = lax.broadcasted_iota(jnp.int32, (nc, CHUNK, CHUNK), 2)
    causal = ii >= jj
    diag = ii == jj
    b_row = jnp.sum(jnp.where(ii <= jj, lf, 0.0), axis=1, keepdims=True)
    lf_row = jnp.sum(jnp.where(diag, lf, 0.0), axis=1, keepdims=True)
    ic_row = jnp.sum(jnp.where(diag, ic, 0.0), axis=1, keepdims=True)
    b_col = jnp.sum(jnp.where(causal, lf_row, 0.0), axis=2, keepdims=True)
    dmat = jnp.where(causal, b_col - b_row + ic_row, -jnp.inf)
    row_max = jnp.max(dmat, axis=2, keepdims=True)

    m_s = m_ref[...]
    m_t_list, m_prev_list = [], []
    for c in range(nc):
        m_prev_list.append(m_s)
        m_tc = jnp.maximum(b_col[c] + m_s, row_max[c])
        m_t_list.append(m_tc)
        m_s = m_tc[CHUNK - 1:]
    m_ref[...] = m_s

    q = q_ref[0]
    v = v_ref[0]
    kb = k_ref[0] * jnp.asarray(MLSTM_DH ** -0.5, BF16)
    m_t = jnp.stack(m_t_list)
    m_prev = jnp.stack(m_prev_list)
    w_inter = jnp.exp(b_col + m_prev - m_t)
    b_last = b_col[:, CHUNK - 1:, :]
    m_new = m_t[:, CHUNK - 1:, :]
    w_k = jnp.exp(b_last - b_col + ic - m_new)
    decay = jnp.exp(b_last + m_prev - m_new)
    p = jnp.exp(dmat - m_t)
    kwb = kb.reshape(nc, CHUNK, MLSTM_DH) * w_k.astype(BF16)
    ones = jnp.ones((V7X_SUBLANES, CHUNK), BF16)

    chunks = [slice(c * CHUNK, (c + 1) * CHUNK) for c in range(nc)]
    qk = [_dot_nt(q[sl], kb[sl]) for sl in chunks]
    kv = [_dot_tn(kwb[c], v[sl]) for c, sl in enumerate(chunks)]
    n_add = [_dot(ones, kwb[c]) for c in range(nc)]
    s = [qk[c] * p[c] for c in range(nc)]
    intra = [_dot(s[c].astype(BF16), v[sl]) for c, sl in enumerate(chunks)]
    c_s = c_ref[...]
    n_s = n_ref[...]
    c_states, n_states = [], []
    for c in range(nc):
        c_states.append(c_s.astype(BF16))
        n_states.append(n_s)
        c_s = decay[c] * c_s + kv[c]
        n_s = decay[c] * n_s + n_add[c][:1]
    c_ref[...] = c_s
    n_ref[...] = n_s
    inter = [_dot(q[sl], c_states[c]) for c, sl in enumerate(chunks)]
    w_inter = w_inter.reshape(ts, 1)
    num = jnp.concatenate(intra, axis=0) + w_inter * jnp.concatenate(inter, axis=0)
    s_sum = jnp.sum(jnp.stack(s), axis=2, keepdims=True).reshape(ts, 1)
    qn = [_dot_nt(q[sl], jnp.broadcast_to(n_states[c], (V7X_SUBLANES, MLSTM_DH)).astype(BF16))[:, :1]
          for c, sl in enumerate(chunks)]
    den = s_sum + w_inter * jnp.concatenate(qn, axis=0)
    h = num / jnp.maximum(jnp.abs(den), jnp.exp(-m_t.reshape(ts, 1)))
    o = _sigmoid(mo_ref[0].astype(F32)) * h
    o = o * lax.rsqrt(jnp.mean(o * o, axis=-1, keepdims=True) + EPS)
    o_ref[0] = (o * mn_ref[0]).astype(BF16)


def _mlstm(mq, mk, mv, gates, proj, m_norm, l):
    bsz, seq, width = mq.shape
    dh = MLSTM_DH
    ts = min(seq, 2048)
    hspec = lambda: pl.BlockSpec((1, ts, dh), lambda b, h, t: (b, t, h))
    return pl.pallas_call(
        _mlstm_kernel,
        out_shape=jax.ShapeDtypeStruct((bsz, seq, width), BF16),
        grid=(bsz, MLSTM_HEADS, seq // ts),
        in_specs=[
            hspec(), hspec(), hspec(),
            pl.BlockSpec((1, ts, V7X_LANES), lambda b, h, t: (b, t, 0)),
            pl.BlockSpec((1, ts, dh), lambda b, h, t: (b, t, OFF_MO // dh + h)),
            pl.BlockSpec((1, 1, dh), lambda b, h, t: (l, 0, h)),
        ],
        out_specs=hspec(),
        scratch_shapes=[pltpu.VMEM((dh, dh), F32), pltpu.VMEM((1, dh), F32), pltpu.VMEM((1, 1), F32)],
        compiler_params=_cparams(("parallel", "parallel", "arbitrary")),
        name="mlstm",
    )(mq, mk, mv, gates, proj, m_norm)


def _merge_kernel(x_ref, ret_ref, lru_ref, mls_ref, g0_ref, g1_ref, g2_ref, mod_ref,
                  wr_ref, wl_ref, wm_ref, wo_ref, w1f_ref, w2f_ref, o_ref, w1b_ref, w2b_ref):
    w1b_ref[...] = w1f_ref[0].astype(BF16)
    w2b_ref[...] = w2f_ref[0].astype(BF16)
    merged = (_sigmoid(g0_ref[0].astype(F32)) * _dot(ret_ref[0], wr_ref[...])
              + _sigmoid(g1_ref[0].astype(F32)) * _dot(lru_ref[0], wl_ref[...])
              + _sigmoid(g2_ref[0].astype(F32)) * _dot(mls_ref[0], wm_ref[...]))
    y = _dot(merged.astype(BF16), wo_ref[...])
    o_ref[0] = x_ref[0] + mod_ref[0, 0, 2:3, :] * y


def _merge(x, ret, lru, mls, proj, mod, w_br_ret, w_br_lru, w_br_mlstm, w_out, w_ff1, w_ff2, l):
    bsz, seq, d = x.shape
    dff = w_ff1.shape[-1]
    tm = min(seq, 512)
    nt = seq // tm
    steps = bsz * nt
    r1, r2 = d // steps, dff // steps
    gb = OFF_GATE // d
    tok = lambda: pl.BlockSpec((1, tm, d), lambda b, i: (b, i, 0))
    wsp = lambda: pl.BlockSpec((d, d), lambda b, i: (0, 0))
    return pl.pallas_call(
        _merge_kernel,
        out_shape=(jax.ShapeDtypeStruct((bsz, seq, d), F32),
                   jax.ShapeDtypeStruct((d, dff), BF16), jax.ShapeDtypeStruct((dff, d), BF16)),
        grid=(bsz, nt),
        in_specs=[
            tok(), tok(), tok(), tok(),
            pl.BlockSpec((1, tm, d), lambda b, i: (b, i, gb)),
            pl.BlockSpec((1, tm, d), lambda b, i: (b, i, gb + 1)),
            pl.BlockSpec((1, tm, d), lambda b, i: (b, i, gb + 2)),
            pl.BlockSpec((1, 1, 6, d), lambda b, i: (l, b, 0, 0)),
            wsp(), wsp(), wsp(), wsp(),
            pl.BlockSpec((1, r1, dff), lambda b, i: (l, b * nt + i, 0)),
            pl.BlockSpec((1, r2, d), lambda b, i: (l, b * nt + i, 0)),
        ],
        out_specs=(tok(),
                   pl.BlockSpec((r1, dff), lambda b, i: (b * nt + i, 0)),
                   pl.BlockSpec((r2, d), lambda b, i: (b * nt + i, 0))),
        compiler_params=_cparams(("parallel", "parallel")),
        name="merge_out",
    )(x, ret, lru, mls, proj, proj, proj, mod, w_br_ret, w_br_lru, w_br_mlstm, w_out, w_ff1, w_ff2)


FFN_HIDDEN_SPLIT = 4


def _ffn_kernel(x_ref, mod_ref, gain_ref, w1_ref, w2_ref, *rest, last):
    x = x_ref[0]
    h = _modulated_norm(x, gain_ref[0], mod_ref[0, 0, 4:5, :], mod_ref[0, 0, 3:4, :])
    hb = h.astype(BF16)
    part = w1_ref.shape[1] // FFN_HIDDEN_SPLIT
    ff = None
    for n in range(FFN_HIDDEN_SPLIT):
        cs = slice(n * part, (n + 1) * part)
        a = jnp.square(jnp.maximum(_dot(hb, w1_ref[:, cs]), 0.0))
        t = _dot(a.astype(BF16), w2_ref[cs, :])
        ff = t if ff is None else ff + t
    y = x + mod_ref[0, 0, 5:6, :] * ff
    if last:
        fgain_ref, o_ref = rest
        o_ref[0] = y * lax.rsqrt(jnp.mean(y * y, axis=-1, keepdims=True) + EPS) * fgain_ref[...]
    else:
        nmod_ref, ngain_ref, o_ref, hn_ref = rest
        o_ref[0] = y
        hn = _modulated_norm(y, ngain_ref[0], nmod_ref[0, 0, 1:2, :], nmod_ref[0, 0, 0:1, :])
        hn_ref[0] = hn.astype(BF16)


def _ffn(x, mod, gain, w1, w2, mix_gain, final_gain, l, last):
    bsz, seq, d = x.shape
    dff = w1.shape[-1]
    tm = min(seq, 1024)
    tok = lambda: pl.BlockSpec((1, tm, d), lambda b, i: (b, i, 0))
    in_specs = [
        tok(),
        pl.BlockSpec((1, 1, 6, d), lambda b, i: (l, b, 0, 0)),
        pl.BlockSpec((1, 1, d), lambda b, i: (l, 0, 0)),
        pl.BlockSpec((d, dff), lambda b, i: (0, 0), pipeline_mode=pl.Buffered(1)),
        pl.BlockSpec((dff, d), lambda b, i: (0, 0), pipeline_mode=pl.Buffered(1)),
    ]
    if last:
        in_specs.append(pl.BlockSpec((1, d), lambda b, i: (0, 0)))
        extra = (final_gain,)
        out_shape = jax.ShapeDtypeStruct((bsz, seq, d), F32)
        out_specs = tok()
    else:
        in_specs += [pl.BlockSpec((1, 1, 6, d), lambda b, i: (l + 1, b, 0, 0)),
                     pl.BlockSpec((1, 1, d), lambda b, i: (l + 1, 0, 0))]
        extra = (mod, mix_gain)
        out_shape = (jax.ShapeDtypeStruct((bsz, seq, d), F32), jax.ShapeDtypeStruct((bsz, seq, d), BF16))
        out_specs = (tok(), tok())
    return pl.pallas_call(
        functools.partial(_ffn_kernel, last=last),
        out_shape=out_shape,
        grid=(bsz, seq // tm),
        in_specs=in_specs,
        out_specs=out_specs,
        compiler_params=_cparams(("parallel", "parallel")),
        name="ffn",
    )(x, mod, gain, w1, w2, *extra)


def _block_diag_tiles(w, tile):
    depth, nb, bs, _ = w.shape
    rows = nb * bs
    sel = jnp.tile(jnp.eye(bs, dtype=w.dtype), (1, tile // bs))
    dense = jnp.einsum('lre,ec->lrc', w.reshape(depth, rows, bs), sel, precision=lax.Precision.HIGHEST)
    r = (np.arange(rows) % tile) // bs
    c = np.arange(tile) // bs
    dense = jnp.where(jnp.asarray(r[:, None] == c[None, :]), dense, 0.0)
    return dense.reshape(depth, rows // tile, tile, tile)


def kernel(x, c, positions, w_ada, b_ada, norm_mix, norm_mlp, w_in, lru_conv_w, lru_conv_b, lru_w_r, lru_b_r, lru_w_i, lru_b_i, lru_lambda, m_conv_w, m_conv_b, m_w_q, m_w_k, m_w_v, m_w_if, m_b_if, m_norm, w_br_ret, w_br_lru, w_br_mlstm, w_out, w_ff1, w_ff2, final_norm):
    depth = w_in.shape[0]
    bsz, seq, d = x.shape
    nh, dh = MLSTM_HEADS, MLSTM_DH

    mod = _ada(c, w_ada, b_ada).reshape(depth, bsz, 6, d)
    cc, ss = _rope_tables(positions)

    vec = lambda a: a.reshape(depth, 1, a.shape[-1])
    w_r_t = (0.5 * _block_diag_tiles(lru_w_r, V7X_LANES)).astype(BF16)
    w_i_t = (0.5 * _block_diag_tiles(lru_w_i, V7X_LANES)).astype(BF16)
    wq_t = _block_diag_tiles(m_w_q, dh).astype(BF16)
    wk_t = _block_diag_tiles(m_w_k, dh).astype(BF16)
    wv_t = _block_diag_tiles(m_w_v, dh).astype(BF16)
    wif_t = jnp.pad(m_w_if, ((0, 0), (0, 0), (0, V7X_LANES - 2 * nh))).reshape(depth, 3 * nh, dh, V7X_LANES).astype(BF16)
    bif_t = jnp.pad(m_b_if, ((0, 0), (0, V7X_LANES - 2 * nh))).reshape(depth, 1, V7X_LANES)
    g_mix, g_mlp = vec(norm_mix), vec(norm_mlp)
    l_cb, l_br, l_bi, l_lam = vec(lru_conv_b), vec(0.5 * lru_b_r), vec(0.5 * lru_b_i), vec(lru_lambda)
    m_cb, m_nrm = vec(m_conv_b), vec(m_norm)
    f_gain = final_norm.reshape(1, d)

    h = _prenorm(x, mod, g_mix, 0)
    for l in range(depth):
        proj = _inproj(h, w_in, l)
        ret = _retention(proj, cc, ss)
        lru = _lru(proj, lru_conv_w, l_cb, w_r_t, l_br, w_i_t, l_bi, l_lam, l)
        mq, mk, mv, gates, w_br_ret_b, w_br_lru_b, w_br_mls_b, w_out_b = _mlstm_prep(
            proj, m_conv_w, m_cb, wq_t, wk_t, wv_t, wif_t, bif_t, (w_br_ret, w_br_lru, w_br_mlstm, w_out), l)
        mls = _mlstm(mq, mk, mv, gates, proj, m_nrm, l)
        x, w_ff1_b, w_ff2_b = _merge(x, ret, lru, mls, proj, mod, w_br_ret_b, w_br_lru_b, w_br_mls_b, w_out_b,
                                     w_ff1, w_ff2, l)
        if l == depth - 1:
            return _ffn(x, mod, g_mlp, w_ff1_b, w_ff2_b, g_mix, f_gain, l, last=True)
        x, h = _ffn(x, mod, g_mlp, w_ff1_b, w_ff2_b, g_mix, f_gain, l, last=False)
```

```python
import functools

import jax
import jax.numpy as jnp
import numpy as np
from jax import lax
from jax.experimental import pallas as pl
from jax.experimental.pallas import tpu as pltpu

F32 = jnp.float32
BF16 = jnp.bfloat16

D_MODEL = 1024
RET_HEADS = 4
RET_DK = 128
RET_DV = 256
ROPE_BASE = 10000.0
LRU_C = 8.0
CONV_W = 4
MLSTM_HEADS = 4
MLSTM_DH = 256
CHUNK = 128
EPS = 1e-6

OFF_RQ, OFF_RK, OFF_RV, OFF_RG = 0, 512, 1024, 2048
OFF_LX, OFF_LY, OFF_MX, OFF_MO, OFF_GATE = 3072, 4096, 5120, 6144, 7168

V7X_LANES = 128
V7X_SUBLANES = 8
V7X_VMEM_LIMIT_BYTES = 56 * 1024 * 1024


def _cparams(sem):
    return pltpu.CompilerParams(dimension_semantics=sem, vmem_limit_bytes=V7X_VMEM_LIMIT_BYTES)


def _dot(a, b):
    return jnp.dot(a, b, preferred_element_type=F32)


def _dot_nt(a, b):
    return lax.dot_general(a, b, (((1,), (1,)), ((), ())), preferred_element_type=F32)


def _dot_tn(a, b):
    return lax.dot_general(a, b, (((0,), (0,)), ((), ())), preferred_element_type=F32)


def _sigmoid(x):
    return 0.5 * jnp.tanh(0.5 * x) + 0.5


def _silu(x):
    return x * _sigmoid(x)


def _ada_kernel(c_ref, w_ref, b_ref, o_ref):
    cond = _silu(c_ref[...])
    o_ref[0] = _dot(cond.astype(BF16), w_ref[0].astype(BF16)) + b_ref[0]


def _ada(c, w_ada, b_ada):
    depth, d, n = w_ada.shape
    bsz = c.shape[0]
    tn = 2048
    return pl.pallas_call(
        _ada_kernel,
        out_shape=jax.ShapeDtypeStruct((depth, bsz, n), F32),
        grid=(depth, n // tn),
        in_specs=[
            pl.BlockSpec((bsz, d), lambda l, j: (0, 0)),
            pl.BlockSpec((1, d, tn), lambda l, j: (l, 0, j)),
            pl.BlockSpec((1, 1, tn), lambda l, j: (l, 0, j)),
        ],
        out_specs=pl.BlockSpec((1, bsz, tn), lambda l, j: (l, 0, j)),
        compiler_params=_cparams(("parallel", "parallel")),
        name="ada_mod",
    )(c, w_ada, b_ada.reshape(depth, 1, n))


def _rope_kernel(pos_ref, invf_ref, cc_ref, ss_ref):
    half = RET_DK // 2
    rows = pos_ref.shape[1] // 2
    lo = lax.broadcasted_iota(jnp.int32, (rows, RET_DK), 1) < half
    ang = jnp.where(lo, pos_ref[0, :rows], pos_ref[0, rows:]) * invf_ref[...]
    c, s = jnp.cos(ang), jnp.sin(ang)
    c_sw, s_sw = pltpu.roll(c, half, 1), pltpu.roll(s, half, 1)
    cc_ref[0, :rows] = jnp.where(lo, c, c_sw)
    cc_ref[0, rows:] = jnp.where(lo, c_sw, c)
    ss_ref[0, :rows] = jnp.where(lo, -s, s_sw)
    ss_ref[0, rows:] = jnp.where(lo, -s_sw, s)


def _rope_tables(positions):
    bsz, seq = positions.shape
    half = RET_DK // 2
    inv_freq = ROPE_BASE ** (-jnp.arange(half, dtype=F32) / half)
    invf = jnp.concatenate([inv_freq, inv_freq]).reshape(1, RET_DK)
    pos = positions.astype(F32).reshape(bsz, seq, 1)
    ts = min(seq, 512)
    out = jax.ShapeDtypeStruct((bsz, seq, RET_DK), F32)
    return pl.pallas_call(
        _rope_kernel,
        out_shape=(out, out),
        grid=(bsz, seq // ts),
        in_specs=[
            pl.BlockSpec((1, ts, 1), lambda b, t: (b, t, 0)),
            pl.BlockSpec((1, RET_DK), lambda b, t: (0, 0)),
        ],
        out_specs=(pl.BlockSpec((1, ts, RET_DK), lambda b, t: (b, t, 0)),
                   pl.BlockSpec((1, ts, RET_DK), lambda b, t: (b, t, 0))),
        compiler_params=_cparams(("parallel", "parallel")),
        name="rope_tables",
    )(pos, invf)


def _modulated_norm(x, gain, scale, shift):
    xn = x * lax.rsqrt(jnp.mean(x * x, axis=-1, keepdims=True) + EPS)
    return xn * gain * (1.0 + scale) + shift


def _prenorm_kernel(x_ref, mod_ref, gain_ref, o_ref):
    h = _modulated_norm(x_ref[0], gain_ref[0], mod_ref[0, 0, 1:2, :], mod_ref[0, 0, 0:1, :])
    o_ref[0] = h.astype(BF16)


def _prenorm(x, mod, gain, l):
    bsz, seq, d = x.shape
    tm = min(seq, 2048)
    tok = lambda: pl.BlockSpec((1, tm, d), lambda b, i: (b, i, 0))
    return pl.pallas_call(
        _prenorm_kernel,
        out_shape=jax.ShapeDtypeStruct((bsz, seq, d), BF16),
        grid=(bsz, seq // tm),
        in_specs=[
            tok(),
            pl.BlockSpec((1, 1, 6, d), lambda b, i: (l, b, 0, 0)),
            pl.BlockSpec((1, 1, d), lambda b, i: (l, 0, 0)),
        ],
        out_specs=tok(),
        compiler_params=_cparams(("parallel", "parallel")),
        name="prenorm",
    )(x, mod, gain)


INPROJ_N_SPLIT = 2


def _inproj_kernel(h_ref, w_ref, o_ref, wb_ref):
    @pl.when((pl.program_id(1) == 0) & (pl.program_id(2) == 0))
    def _():
        wb_ref[...] = w_ref[0].astype(BF16)

    part = wb_ref.shape[1] // INPROJ_N_SPLIT
    for n in range(INPROJ_N_SPLIT):
        cs = slice(n * part, (n + 1) * part)
        o_ref[0, :, cs] = _dot(h_ref[0], wb_ref[:, cs]).astype(BF16)


def _inproj(h, w_in, l):
    bsz, seq, d = h.shape
    n = w_in.shape[-1]
    tm = min(seq, 2048)
    tn = 2048
    return pl.pallas_call(
        _inproj_kernel,
        out_shape=jax.ShapeDtypeStruct((bsz, seq, n), BF16),
        grid=(n // tn, bsz, seq // tm),
        in_specs=[
            pl.BlockSpec((1, tm, d), lambda j, b, i: (b, i, 0)),
            pl.BlockSpec((1, d, tn), lambda j, b, i: (l, 0, j)),
        ],
        out_specs=pl.BlockSpec((1, tm, tn), lambda j, b, i: (b, i, j)),
        scratch_shapes=[pltpu.VMEM((d, tn), BF16)],
        compiler_params=_cparams(("arbitrary", "arbitrary", "arbitrary")),
        name="in_proj",
    )(h, w_in)


def _ret_kernel(q_ref, k_ref, v_ref, g_ref, cc_ref, ss_ref, o_ref, state_ref):
    head = pl.program_id(1)

    @pl.when(pl.program_id(2) == 0)
    def _():
        state_ref[...] = jnp.zeros_like(state_ref)

    hv = jnp.zeros((1, 1), F32) + head.astype(F32)
    log_gamma = jnp.log1p(-jnp.exp2(-5.0 - hv))
    ii = lax.broadcasted_iota(jnp.int32, (CHUNK, CHUNK), 0)
    jj = lax.broadcasted_iota(jnp.int32, (CHUNK, CHUNK), 1)
    causal = ii >= jj
    diff = jnp.where(causal, (ii - jj).astype(F32), 0.0)
    decay_intra = jnp.where(causal, jnp.exp(log_gamma * diff), 0.0)
    pos = lax.broadcasted_iota(jnp.int32, (CHUNK, 1), 0).astype(F32)
    decay_q = jnp.exp(log_gamma * (pos + 1.0))
    decay_k = jnp.exp(log_gamma * (CHUNK - 1.0 - pos))
    decay_chunk = jnp.exp(log_gamma * CHUNK)

    cc = cc_ref[0]
    ss = ss_ref[0]
    qk_lanes = pl.ds(pl.multiple_of(head * RET_DK, RET_DK), RET_DK)
    v_lanes = pl.ds(pl.multiple_of(head * RET_DV, RET_DV), RET_DV)
    q = q_ref[0, :, qk_lanes].astype(F32)
    k = k_ref[0, :, qk_lanes].astype(F32)
    q = q * cc + pltpu.roll(q, RET_DK // 2, 1) * ss
    k = (k * cc + pltpu.roll(k, RET_DK // 2, 1) * ss) * RET_DK ** -0.5

    ts = q.shape[0]
    nc = ts // CHUNK
    qb = q.astype(BF16)
    kb = k.astype(BF16)
    qd = (q.reshape(nc, CHUNK, RET_DK) * decay_q).astype(BF16)
    kd = (k.reshape(nc, CHUNK, RET_DK) * decay_k).astype(BF16)
    v = v_ref[0, :, v_lanes]
    chunks = [slice(c * CHUNK, (c + 1) * CHUNK) for c in range(nc)]
    scores = [_dot_nt(qb[sl], kb[sl]) for sl in chunks]
    kv = [_dot_tn(kd[c], v[sl]) for c, sl in enumerate(chunks)]
    scores = [(s * decay_intra).astype(BF16) for s in scores]
    intra = [_dot(scores[c], v[sl]) for c, sl in enumerate(chunks)]
    state = state_ref[...]
    states = []
    for c in range(nc):
        states.append(state.astype(BF16))
        state = decay_chunk * state + kv[c]
    state_ref[...] = state
    inter = [_dot(qd[c], states[c]) for c in range(nc)]
    out = jnp.concatenate(intra, axis=0) + jnp.concatenate(inter, axis=0)
    out = out * lax.rsqrt(jnp.mean(out * out, axis=-1, keepdims=True) + EPS)
    o_ref[0] = (out * _silu(g_ref[0, :, v_lanes].astype(F32))).astype(BF16)


def _retention(proj, cc, ss):
    bsz, seq, _ = proj.shape
    ts = min(seq, 2048)
    wqk, wv = RET_HEADS * RET_DK, RET_HEADS * RET_DV
    return pl.pallas_call(
        _ret_kernel,
        out_shape=jax.ShapeDtypeStruct((bsz, seq, RET_HEADS * RET_DV), BF16),
        grid=(bsz, RET_HEADS, seq // ts),
        in_specs=[
            pl.BlockSpec((1, ts, wqk), lambda b, h, t: (b, t, OFF_RQ // wqk)),
            pl.BlockSpec((1, ts, wqk), lambda b, h, t: (b, t, OFF_RK // wqk)),
            pl.BlockSpec((1, ts, wv), lambda b, h, t: (b, t, OFF_RV // wv)),
            pl.BlockSpec((1, ts, wv), lambda b, h, t: (b, t, OFF_RG // wv)),
            pl.BlockSpec((1, ts, RET_DK), lambda b, h, t: (b, t, 0)),
            pl.BlockSpec((1, ts, RET_DK), lambda b, h, t: (b, t, 0)),
        ],
        out_specs=pl.BlockSpec((1, ts, RET_DV), lambda b, h, t: (b, t, h)),
        scratch_shapes=[pltpu.VMEM((RET_DK, RET_DV), F32)],
        compiler_params=_cparams(("parallel", "parallel", "arbitrary")),
        name="retention",
    )(proj, proj, proj, proj, cc, ss)


def _causal_conv(xs_ref, rows, w, b):
    y = b
    for k in range(CONV_W):
        s = CONV_W - 1 - k
        y = y + w[k:k + 1, :] * xs_ref[pl.ds(V7X_SUBLANES - s, rows), :]
    return y


LRU_COLS = 512
LRU_SEGS = V7X_SUBLANES


def _lru_pitch(seg_len):
    assert seg_len % V7X_SUBLANES == 0
    return seg_len + 4


def _lru_kernel(lx_ref, ly_ref, cw_ref, cb_ref, wr_ref, br_ref, wi_ref, bi_ref, lam_ref, o_ref,
                xs_ref, a_ref, u_ref, h_ref, p_ref):
    seq = lx_ref.shape[1]
    nslab = lx_ref.shape[2] // V7X_LANES
    seg_len = seq // LRU_SEGS
    pitch = _lru_pitch(seg_len)
    zeros8 = jnp.zeros((V7X_SUBLANES, V7X_LANES), F32)

    for s in range(nslab):
        cs = slice(s * V7X_LANES, (s + 1) * V7X_LANES)
        xs_ref[s, pl.ds(0, V7X_SUBLANES), :] = zeros8
        xs_ref[s, pl.ds(V7X_SUBLANES, seq), :] = lx_ref[0, :, cs].astype(F32)
        xl = _causal_conv(xs_ref.at[s], seq, cw_ref[0, :, cs], cb_ref[0, :, cs])
        xb = xl.astype(BF16)
        tr = jnp.tanh(_dot(xb, wr_ref[0, s]) + br_ref[0, :, cs])
        ti = jnp.tanh(_dot(xb, wi_ref[0, s]) + bi_ref[0, :, cs])
        lam = lam_ref[0, :, cs]
        softplus_neg_lam = jnp.maximum(-lam, 0.0) + jnp.log1p(jnp.exp(-jnp.abs(lam)))
        log_a = (tr + 1.0) * (-0.5 * LRU_C * softplus_neg_lam)
        a = jnp.exp(log_a)
        y = -jnp.tanh(log_a) * (a * a + 1.0)
        u = jnp.where(y > 0.0, y * lax.rsqrt(y), 0.0) * ((0.5 * ti + 0.5) * xl)
        for g in range(LRU_SEGS):
            a_ref[s, pl.ds(g * pitch, seg_len), :] = a[g * seg_len:(g + 1) * seg_len]
            u_ref[s, pl.ds(g * pitch, seg_len), :] = u[g * seg_len:(g + 1) * seg_len]

    def body(t, carry):
        hs, ps = carry
        new_h, new_p = [], []
        for s in range(nslab):
            rows = pl.ds(t, LRU_SEGS, stride=pitch)
            av = a_ref[s, rows, :]
            h = av * hs[s] + u_ref[s, rows, :]
            p = av * ps[s]
            h_ref[s, rows, :] = h
            p_ref[s, rows, :] = p
            new_h.append(h)
            new_p.append(p)
        return tuple(new_h), tuple(new_p)

    init = (tuple(zeros8 for _ in range(nslab)), tuple(zeros8 + 1.0 for _ in range(nslab)))
    h_end, p_end = lax.fori_loop(0, seg_len, body, init, unroll=8)

    for s in range(nslab):
        cs = slice(s * V7X_LANES, (s + 1) * V7X_LANES)
        carry = jnp.zeros((1, V7X_LANES), F32)
        for g in range(LRU_SEGS):
            rows = pl.ds(g * pitch, seg_len)
            h = h_ref[s, rows, :] + p_ref[s, rows, :] * carry
            gate = jax.nn.gelu(ly_ref[0, g * seg_len:(g + 1) * seg_len, cs].astype(F32))
            o_ref[0, g * seg_len:(g + 1) * seg_len, cs] = (h * gate).astype(BF16)
            carry = h_end[s][g:g + 1] + p_end[s][g:g + 1] * carry


def _lru(proj, conv_w, conv_b, w_r, b_r, w_i, b_i, lam, l):
    bsz, seq, _ = proj.shape
    w = LRU_COLS
    nslab = w // V7X_LANES
    scan_rows = LRU_SEGS * _lru_pitch(seq // LRU_SEGS)
    vec = lambda: pl.BlockSpec((1, 1, w), lambda b, j: (l, 0, j))
    wsp = lambda: pl.BlockSpec((1, nslab, V7X_LANES, V7X_LANES), lambda b, j: (l, j, 0, 0))
    return pl.pallas_call(
        _lru_kernel,
        out_shape=jax.ShapeDtypeStruct((bsz, seq, D_MODEL), BF16),
        grid=(bsz, D_MODEL // w),
        in_specs=[
            pl.BlockSpec((1, seq, w), lambda b, j: (b, 0, OFF_LX // w + j)),
            pl.BlockSpec((1, seq, w), lambda b, j: (b, 0, OFF_LY // w + j)),
            pl.BlockSpec((1, CONV_W, w), lambda b, j: (l, 0, j)),
            vec(), wsp(), vec(), wsp(), vec(), vec(),
        ],
        out_specs=pl.BlockSpec((1, seq, w), lambda b, j: (b, 0, j)),
        scratch_shapes=[pltpu.VMEM((nslab, seq + V7X_SUBLANES, V7X_LANES), F32)]
                       + [pltpu.VMEM((nslab, scan_rows, V7X_LANES), F32)] * 4,
        compiler_params=_cparams(("parallel", "parallel")),
        name="rg_lru",
    )(proj, proj, conv_w, conv_b, w_r, b_r, w_i, b_i, lam)


def _mprep_kernel(mx_ref, cw_ref, cb_ref, wq_ref, wk_ref, wv_ref, wif_ref, bif_ref, *rest):
    wf_refs, (q_ref, k_ref, v_ref, gate_ref), wb_refs, xs_ref = rest[:4], rest[4:8], rest[8:12], rest[12]
    for wf_ref, wb_ref in zip(wf_refs, wb_refs):
        wb_ref[...] = wf_ref[0].astype(BF16)
    ts = mx_ref.shape[1]

    @pl.when(pl.program_id(1) == 0)
    def _():
        for s in range(xs_ref.shape[0]):
            xs_ref[s, pl.ds(ts, V7X_SUBLANES), :] = jnp.zeros((V7X_SUBLANES, V7X_LANES), F32)

    mxb = mx_ref[0]
    xc = []
    for s in range(xs_ref.shape[0]):
        cs = slice(s * V7X_LANES, (s + 1) * V7X_LANES)
        xs_ref[s, pl.ds(0, V7X_SUBLANES), :] = xs_ref[s, pl.ds(ts, V7X_SUBLANES), :]
        xs_ref[s, pl.ds(V7X_SUBLANES, ts), :] = mxb[:, cs].astype(F32)
        y = _causal_conv(xs_ref.at[s], ts, cw_ref[0, :, cs], cb_ref[0, :, cs])
        xc.append(_silu(y).astype(BF16))
    xc = jnp.concatenate(xc, axis=1)
    acc = jnp.zeros((ts, V7X_LANES), F32) + bif_ref[0]
    nh = MLSTM_HEADS
    for h in range(nh):
        cs = slice(h * MLSTM_DH, (h + 1) * MLSTM_DH)
        mq = _dot(xc[:, cs], wq_ref[0, h]).astype(BF16)
        mk = _dot(xc[:, cs], wk_ref[0, h]).astype(BF16)
        mv = _dot(mxb[:, cs], wv_ref[0, h]).astype(BF16)
        q_ref[0, :, cs] = mq
        k_ref[0, :, cs] = mk
        v_ref[0, :, cs] = mv
        acc = acc + _dot(mq, wif_ref[0, h]) + _dot(mk, wif_ref[0, nh + h]) + _dot(mv, wif_ref[0, 2 * nh + h])
    lane = lax.broadcasted_iota(jnp.int32, acc.shape, 1)
    log_f = jnp.minimum(acc, 0.0) - jnp.log1p(jnp.exp(-jnp.abs(acc)))
    gate_ref[0] = jnp.where(lane >= nh, log_f, acc)


def _mlstm_prep(proj, conv_w, conv_b, wq, wk, wv, wif, bif, merge_ws, l):
    bsz, seq, _ = proj.shape
    width = MLSTM_HEADS * MLSTM_DH
    ts = min(seq, 1024)
    nt = seq // ts
    nh, dh = MLSTM_HEADS, MLSTM_DH
    d = merge_ws[0].shape[-1]
    rows = merge_ws[0].shape[1] // (bsz * nt)
    qkv = jax.ShapeDtypeStruct((bsz, seq, width), BF16)
    wb_shape = jax.ShapeDtypeStruct(merge_ws[0].shape[1:], BF16)
    wspec = lambda: pl.BlockSpec((1, nh, dh, dh), lambda b, t: (l, 0, 0, 0))
    ospec = lambda: pl.BlockSpec((1, ts, width), lambda b, t: (b, t, 0))
    return pl.pallas_call(
        _mprep_kernel,
        out_shape=(qkv, qkv, qkv, jax.ShapeDtypeStruct((bsz, seq, V7X_LANES), F32)) + (wb_shape,) * 4,
        grid=(bsz, nt),
        in_specs=[
            pl.BlockSpec((1, ts, width), lambda b, t: (b, t, OFF_MX // width)),
            pl.BlockSpec((1, CONV_W, width), lambda b, t: (l, 0, 0)),
            pl.BlockSpec((1, 1, width), lambda b, t: (l, 0, 0)),
            wspec(), wspec(), wspec(),
            pl.BlockSpec((1, 3 * nh, dh, V7X_LANES), lambda b, t: (l, 0, 0, 0)),
            pl.BlockSpec((1, 1, V7X_LANES), lambda b, t: (l, 0, 0)),
        ] + [pl.BlockSpec((1, rows, d), lambda b, t: (l, b * nt + t, 0))] * 4,
        out_specs=(ospec(), ospec(), ospec(),
                   pl.BlockSpec((1, ts, V7X_LANES), lambda b, t: (b, t, 0)))
                  + (pl.BlockSpec((rows, d), lambda b, t: (b * nt + t, 0)),) * 4,
        scratch_shapes=[pltpu.VMEM((width // V7X_LANES, ts + V7X_SUBLANES, V7X_LANES), F32)],
        compiler_params=_cparams(("parallel", "arbitrary")),
        name="mlstm_prep",
    )(proj, conv_w, conv_b, wq, wk, wv, wif, bif, *merge_ws)


def _mlstm_kernel(q_ref, k_ref, v_ref, gate_ref, mo_ref, mn_ref, o_ref, c_ref, n_ref, m_ref):
    head = pl.program_id(1)

    @pl.when(pl.program_id(2) == 0)
    def _():
        c_ref[...] = jnp.zeros_like(c_ref)
        n_ref[...] = jnp.zeros_like(n_ref)
        m_ref[...] = jnp.zeros_like(m_ref)

    gates = gate_ref[0]
    ts = gates.shape[0]
    nc = ts // CHUNK
    lane = lax.broadcasted_iota(jnp.int32, gates.shape, 1)
    ic = jnp.sum(jnp.where(lane == head, gates, 0.0), axis=1, keepdims=True).reshape(nc, CHUNK, 1)
    lf = jnp.sum(jnp.where(lane == head + MLSTM_HEADS, gates, 0.0), axis=1, keepdims=True).reshape(nc, CHUNK, 1)

    ii = lax.broadcasted_iota(jnp.int32, (nc, CHUNK, CHUNK), 1)
    jj = lax.broadcasted_iota(jnp.int32, (nc, CHUNK, CHUNK), 2)
    causal = ii >= jj
    diag = ii == jj
    b_row = jnp.sum(jnp.where(ii <= jj, lf, 0.0), axis=1, keepdims=True)
    lf_row = jnp.sum(jnp.where(diag, lf, 0.0), axis=1, keepdims=True)
    ic_row = jnp.sum(jnp.where(diag, ic, 0.0), axis=1, keepdims=True)
    b_col = jnp.sum(jnp.where(causal, lf_row, 0.0), axis=2, keepdims=True)
    dmat = jnp.where(causal, b_col - b_row + ic_row, -jnp.inf)
    row_max = jnp.max(dmat, axis=2, keepdims=True)

    m_s = m_ref[...]
    m_t_list, m_prev_list = [], []
    for c in range(nc):
        m_prev_list.append(m_s)
        m_tc = jnp.maximum(b_col[c] + m_s, row_max[c])
        m_t_list.append(m_tc)
        m_s = m_tc[CHUNK - 1:]
    m_ref[...] = m_s

    lanes = pl.ds(pl.multiple_of(head * MLSTM_DH, MLSTM_DH), MLSTM_DH)
    q = q_ref[0, :, lanes]
    v = v_ref[0, :, lanes]
    kb = k_ref[0, :, lanes] * jnp.asarray(MLSTM_DH ** -0.5, BF16)
    m_t = jnp.stack(m_t_list)
    m_prev = jnp.stack(m_prev_list)
    w_inter = jnp.exp(b_col + m_prev - m_t)
    b_last = b_col[:, CHUNK - 1:, :]
    m_new = m_t[:, CHUNK - 1:, :]
    w_k = jnp.exp(b_last - b_col + ic - m_new)
    decay = jnp.exp(b_last + m_prev - m_new)
    p = jnp.exp(dmat - m_t)
    kwb = kb.reshape(nc, CHUNK, MLSTM_DH) * w_k.astype(BF16)
    ones = jnp.ones((V7X_SUBLANES, CHUNK), BF16)

    chunks = [slice(c * CHUNK, (c + 1) * CHUNK) for c in range(nc)]
    qk = [_dot_nt(q[sl], kb[sl]) for sl in chunks]
    kv = [_dot_tn(kwb[c], v[sl]) for c, sl in enumerate(chunks)]
    n_add = [_dot(ones, kwb[c]) for c in range(nc)]
    s = [qk[c] * p[c] for c in range(nc)]
    intra = [_dot(s[c].astype(BF16), v[sl]) for c, sl in enumerate(chunks)]
    c_s = c_ref[...]
    n_s = n_ref[...]
    c_states, n_states = [], []
    for c in range(nc):
        c_states.append(c_s.astype(BF16))
        n_states.append(n_s)
        c_s = decay[c] * c_s + kv[c]
        n_s = decay[c] * n_s + n_add[c][:1]
    c_ref[...] = c_s
    n_ref[...] = n_s
    inter = [_dot(q[sl], c_states[c]) for c, sl in enumerate(chunks)]
    w_inter = w_inter.reshape(ts, 1)
    num = jnp.concatenate(intra, axis=0) + w_inter * jnp.concatenate(inter, axis=0)
    s_sum = jnp.sum(jnp.stack(s), axis=2, keepdims=True).reshape(ts, 1)
    qn = [_dot_nt(q[sl], jnp.broadcast_to(n_states[c], (V7X_SUBLANES, MLSTM_DH)).astype(BF16))[:, :1]
          for c, sl in enumerate(chunks)]
    den = s_sum + w_inter * jnp.concatenate(qn, axis=0)
    h = num / jnp.maximum(jnp.abs(den), jnp.exp(-m_t.reshape(ts, 1)))
    o = _sigmoid(mo_ref[0, :, lanes].astype(F32)) * h
    o = o * lax.rsqrt(jnp.mean(o * o, axis=-1, keepdims=True) + EPS)
    o_ref[0] = (o * mn_ref[0]).astype(BF16)


def _mlstm(mq, mk, mv, gates, proj, m_norm, l):
    bsz, seq, width = mq.shape
    dh = MLSTM_DH
    ts = min(seq, 2048)
    hspec = lambda: pl.BlockSpec((1, ts, dh), lambda b, h, t: (b, t, h))
    aspec = lambda blk: pl.BlockSpec((1, ts, width), lambda b, h, t: (b, t, blk))
    return pl.pallas_call(
        _mlstm_kernel,
        out_shape=jax.ShapeDtypeStruct((bsz, seq, width), BF16),
        grid=(bsz, MLSTM_HEADS, seq // ts),
        in_specs=[
            aspec(0), aspec(0), aspec(0),
            pl.BlockSpec((1, ts, V7X_LANES), lambda b, h, t: (b, t, 0)),
            aspec(OFF_MO // width),
            pl.BlockSpec((1, 1, dh), lambda b, h, t: (l, 0, h)),
        ],
        out_specs=hspec(),
        scratch_shapes=[pltpu.VMEM((dh, dh), F32), pltpu.VMEM((1, dh), F32), pltpu.VMEM((1, 1), F32)],
        compiler_params=_cparams(("parallel", "parallel", "arbitrary")),
        name="mlstm",
    )(mq, mk, mv, gates, proj, m_norm)


def _merge_kernel(x_ref, ret_ref, lru_ref, mls_ref, g0_ref, g1_ref, g2_ref, mod_ref,
                  wr_ref, wl_ref, wm_ref, wo_ref, w1f_ref, w2f_ref, o_ref, w1b_ref, w2b_ref):
    w1b_ref[...] = w1f_ref[0].astype(BF16)
    w2b_ref[...] = w2f_ref[0].astype(BF16)
    merged = (_sigmoid(g0_ref[0].astype(F32)) * _dot(ret_ref[0], wr_ref[...])
              + _sigmoid(g1_ref[0].astype(F32)) * _dot(lru_ref[0], wl_ref[...])
              + _sigmoid(g2_ref[0].astype(F32)) * _dot(mls_ref[0], wm_ref[...]))
    y = _dot(merged.astype(BF16), wo_ref[...])
    o_ref[0] = x_ref[0] + mod_ref[0, 0, 2:3, :] * y


def _merge(x, ret, lru, mls, proj, mod, w_br_ret, w_br_lru, w_br_mlstm, w_out, w_ff1, w_ff2, l):
    bsz, seq, d = x.shape
    dff = w_ff1.shape[-1]
    tm = min(seq, 512)
    nt = seq // tm
    steps = bsz * nt
    r1, r2 = d // steps, dff // steps
    gb = OFF_GATE // d
    tok = lambda: pl.BlockSpec((1, tm, d), lambda b, i: (b, i, 0))
    wsp = lambda: pl.BlockSpec((d, d), lambda b, i: (0, 0))
    return pl.pallas_call(
        _merge_kernel,
        out_shape=(jax.ShapeDtypeStruct((bsz, seq, d), F32),
                   jax.ShapeDtypeStruct((d, dff), BF16), jax.ShapeDtypeStruct((dff, d), BF16)),
        grid=(bsz, nt),
        in_specs=[
            tok(), tok(), tok(), tok(),
            pl.BlockSpec((1, tm, d), lambda b, i: (b, i, gb)),
            pl.BlockSpec((1, tm, d), lambda b, i: (b, i, gb + 1)),
            pl.BlockSpec((1, tm, d), lambda b, i: (b, i, gb + 2)),
            pl.BlockSpec((1, 1, 6, d), lambda b, i: (l, b, 0, 0)),
            wsp(), wsp(), wsp(), wsp(),
            pl.BlockSpec((1, r1, dff), lambda b, i: (l, b * nt + i, 0)),
            pl.BlockSpec((1, r2, d), lambda b, i: (l, b * nt + i, 0)),
        ],
        out_specs=(tok(),
                   pl.BlockSpec((r1, dff), lambda b, i: (b * nt + i, 0)),
                   pl.BlockSpec((r2, d), lambda b, i: (b * nt + i, 0))),
        compiler_params=_cparams(("parallel", "parallel")),
        name="merge_out",
    )(x, ret, lru, mls, proj, proj, proj, mod, w_br_ret, w_br_lru, w_br_mlstm, w_out, w_ff1, w_ff2)


FFN_HIDDEN_SPLIT = 4


def _ffn_kernel(x_ref, mod_ref, gain_ref, w1_ref, w2_ref, *rest, last):
    x = x_ref[0]
    h = _modulated_norm(x, gain_ref[0], mod_ref[0, 0, 4:5, :], mod_ref[0, 0, 3:4, :])
    hb = h.astype(BF16)
    part = w1_ref.shape[1] // FFN_HIDDEN_SPLIT
    ff = None
    for n in range(FFN_HIDDEN_SPLIT):
        cs = slice(n * part, (n + 1) * part)
        a = jnp.square(jnp.maximum(_dot(hb, w1_ref[:, cs]), 0.0))
        t = _dot(a.astype(BF16), w2_ref[cs, :])
        ff = t if ff is None else ff + t
    y = x + mod_ref[0, 0, 5:6, :] * ff
    if last:
        fgain_ref, o_ref = rest
        o_ref[0] = y * lax.rsqrt(jnp.mean(y * y, axis=-1, keepdims=True) + EPS) * fgain_ref[...]
    else:
        nmod_ref, ngain_ref, o_ref, hn_ref = rest
        o_ref[0] = y
        hn = _modulated_norm(y, ngain_ref[0], nmod_ref[0, 0, 1:2, :], nmod_ref[0, 0, 0:1, :])
        hn_ref[0] = hn.astype(BF16)


def _ffn(x, mod, gain, w1, w2, mix_gain, final_gain, l, last):
    bsz, seq, d = x.shape
    dff = w1.shape[-1]
    tm = min(seq, 1024)
    tok = lambda: pl.BlockSpec((1, tm, d), lambda b, i: (b, i, 0))
    in_specs = [
        tok(),
        pl.BlockSpec((1, 1, 6, d), lambda b, i: (l, b, 0, 0)),
        pl.BlockSpec((1, 1, d), lambda b, i: (l, 0, 0)),
        pl.BlockSpec((d, dff), lambda b, i: (0, 0), pipeline_mode=pl.Buffered(1)),
        pl.BlockSpec((dff, d), lambda b, i: (0, 0), pipeline_mode=pl.Buffered(1)),
    ]
    if last:
        in_specs.append(pl.BlockSpec((1, d), lambda b, i: (0, 0)))
        extra = (final_gain,)
        out_shape = jax.ShapeDtypeStruct((bsz, seq, d), F32)
        out_specs = tok()
    else:
        in_specs += [pl.BlockSpec((1, 1, 6, d), lambda b, i: (l + 1, b, 0, 0)),
                     pl.BlockSpec((1, 1, d), lambda b, i: (l + 1, 0, 0))]
        extra = (mod, mix_gain)
        out_shape = (jax.ShapeDtypeStruct((bsz, seq, d), F32), jax.ShapeDtypeStruct((bsz, seq, d), BF16))
        out_specs = (tok(), tok())
    return pl.pallas_call(
        functools.partial(_ffn_kernel, last=last),
        out_shape=out_shape,
        grid=(bsz, seq // tm),
        in_specs=in_specs,
        out_specs=out_specs,
        compiler_params=_cparams(("parallel", "parallel")),
        name="ffn",
    )(x, mod, gain, w1, w2, *extra)


def _block_diag_tiles(w, tile):
    depth, nb, bs, _ = w.shape
    rows = nb * bs
    sel = jnp.tile(jnp.eye(bs, dtype=w.dtype), (1, tile // bs))
    dense = jnp.einsum('lre,ec->lrc', w.reshape(depth, rows, bs), sel, precision=lax.Precision.HIGHEST)
    r = (np.arange(rows) % tile) // bs
    c = np.arange(tile) // bs
    dense = jnp.where(jnp.asarray(r[:, None] == c[None, :]), dense, 0.0)
    return dense.reshape(depth, rows // tile, tile, tile)


def kernel(x, c, positions, w_ada, b_ada, norm_mix, norm_mlp, w_in, lru_conv_w, lru_conv_b, lru_w_r, lru_b_r, lru_w_i, lru_b_i, lru_lambda, m_conv_w, m_conv_b, m_w_q, m_w_k, m_w_v, m_w_if, m_b_if, m_norm, w_br_ret, w_br_lru, w_br_mlstm, w_out, w_ff1, w_ff2, final_norm):
    depth = w_in.shape[0]
    bsz, seq, d = x.shape
    nh, dh = MLSTM_HEADS, MLSTM_DH

    mod = _ada(c, w_ada, b_ada).reshape(depth, bsz, 6, d)
    cc, ss = _rope_tables(positions)

    vec = lambda a: a.reshape(depth, 1, a.shape[-1])
    w_r_t = (0.5 * _block_diag_tiles(lru_w_r, V7X_LANES)).astype(BF16)
    w_i_t = (0.5 * _block_diag_tiles(lru_w_i, V7X_LANES)).astype(BF16)
    wq_t = _block_diag_tiles(m_w_q, dh).astype(BF16)
    wk_t = _block_diag_tiles(m_w_k, dh).astype(BF16)
    wv_t = _block_diag_tiles(m_w_v, dh).astype(BF16)
    wif_t = jnp.pad(m_w_if, ((0, 0), (0, 0), (0, V7X_LANES - 2 * nh))).reshape(depth, 3 * nh, dh, V7X_LANES).astype(BF16)
    bif_t = jnp.pad(m_b_if, ((0, 0), (0, V7X_LANES - 2 * nh))).reshape(depth, 1, V7X_LANES)
    g_mix, g_mlp = vec(norm_mix), vec(norm_mlp)
    l_cb, l_br, l_bi, l_lam = vec(lru_conv_b), vec(0.5 * lru_b_r), vec(0.5 * lru_b_i), vec(lru_lambda)
    m_cb, m_nrm = vec(m_conv_b), vec(m_norm)
    f_gain = final_norm.reshape(1, d)

    h = _prenorm(x, mod, g_mix, 0)
    for l in range(depth):
        proj = _inproj(h, w_in, l)
        ret = _retention(proj, cc, ss)
        lru = _lru(proj, lru_conv_w, l_cb, w_r_t, l_br, w_i_t, l_bi, l_lam, l)
        mq, mk, mv, gates, w_br_ret_b, w_br_lru_b, w_br_mls_b, w_out_b = _mlstm_prep(
            proj, m_conv_w, m_cb, wq_t, wk_t, wv_t, wif_t, bif_t, (w_br_ret, w_br_lru, w_br_mlstm, w_out), l)
        mls = _mlstm(mq, mk, mv, gates, proj, m_nrm, l)
        x, w_ff1_b, w_ff2_b = _merge(x, ret, lru, mls, proj, mod, w_br_ret_b, w_br_lru_b, w_br_mls_b, w_out_b,
                                     w_ff1, w_ff2, l)
        if l == depth - 1:
            return _ffn(x, mod, g_mlp, w_ff1_b, w_ff2_b, g_mix, f_gain, l, last=True)
        x, h = _ffn(x, mod, g_mlp, w_ff1_b, w_ff2_b, g_mix, f_gain, l, last=False)
```

```python
import functools

import jax
import jax.numpy as jnp
import numpy as np
from jax import lax
from jax.experimental import pallas as pl
from jax.experimental.pallas import tpu as pltpu

F32 = jnp.float32
BF16 = jnp.bfloat16

D_MODEL = 1024
RET_HEADS = 4
RET_DK = 128
RET_DV = 256
ROPE_BASE = 10000.0
LRU_C = 8.0
CONV_W = 4
MLSTM_HEADS = 4
MLSTM_DH = 256
CHUNK = 128
EPS = 1e-6

OFF_RQ, OFF_RK, OFF_RV, OFF_RG = 0, 512, 1024, 2048
OFF_LX, OFF_LY, OFF_MX, OFF_MO, OFF_GATE = 3072, 4096, 5120, 6144, 7168

V7X_LANES = 128
V7X_SUBLANES = 8
V7X_VMEM_LIMIT_BYTES = 56 * 1024 * 1024


def _cparams(sem):
    return pltpu.CompilerParams(dimension_semantics=sem, vmem_limit_bytes=V7X_VMEM_LIMIT_BYTES)


def _dot(a, b):
    return jnp.dot(a, b, preferred_element_type=F32)


def _dot_nt(a, b):
    return lax.dot_general(a, b, (((1,), (1,)), ((), ())), preferred_element_type=F32)


def _dot_tn(a, b):
    return lax.dot_general(a, b, (((0,), (0,)), ((), ())), preferred_element_type=F32)


def _sigmoid(x):
    return 0.5 * jnp.tanh(0.5 * x) + 0.5


def _silu(x):
    return x * _sigmoid(x)


def _ada_kernel(c_ref, w_ref, b_ref, o_ref):
    cond = _silu(c_ref[...])
    o_ref[0] = _dot(cond.astype(BF16), w_ref[0].astype(BF16)) + b_ref[0]


def _ada(c, w_ada, b_ada):
    depth, d, n = w_ada.shape
    bsz = c.shape[0]
    tn = 2048
    return pl.pallas_call(
        _ada_kernel,
        out_shape=jax.ShapeDtypeStruct((depth, bsz, n), F32),
        grid=(depth, n // tn),
        in_specs=[
            pl.BlockSpec((bsz, d), lambda l, j: (0, 0)),
            pl.BlockSpec((1, d, tn), lambda l, j: (l, 0, j)),
            pl.BlockSpec((1, 1, tn), lambda l, j: (l, 0, j)),
        ],
        out_specs=pl.BlockSpec((1, bsz, tn), lambda l, j: (l, 0, j)),
        compiler_params=_cparams(("parallel", "parallel")),
        name="ada_mod",
    )(c, w_ada, b_ada.reshape(depth, 1, n))


def _rope_kernel(pos_ref, invf_ref, cc_ref, ss_ref):
    half = RET_DK // 2
    rows = pos_ref.shape[1] // 2
    lo = lax.broadcasted_iota(jnp.int32, (rows, RET_DK), 1) < half
    ang = jnp.where(lo, pos_ref[0, :rows], pos_ref[0, rows:]) * invf_ref[...]
    c, s = jnp.cos(ang), jnp.sin(ang)
    c_sw, s_sw = pltpu.roll(c, half, 1), pltpu.roll(s, half, 1)
    cc_ref[0, :rows] = jnp.where(lo, c, c_sw)
    cc_ref[0, rows:] = jnp.where(lo, c_sw, c)
    ss_ref[0, :rows] = jnp.where(lo, -s, s_sw)
    ss_ref[0, rows:] = jnp.where(lo, -s_sw, s)


def _rope_tables(positions):
    bsz, seq = positions.shape
    half = RET_DK // 2
    inv_freq = ROPE_BASE ** (-jnp.arange(half, dtype=F32) / half)
    invf = jnp.concatenate([inv_freq, inv_freq]).reshape(1, RET_DK)
    pos = positions.astype(F32).reshape(bsz, seq, 1)
    ts = min(seq, 512)
    out = jax.ShapeDtypeStruct((bsz, seq, RET_DK), F32)
    return pl.pallas_call(
        _rope_kernel,
        out_shape=(out, out),
        grid=(bsz, seq // ts),
        in_specs=[
            pl.BlockSpec((1, ts, 1), lambda b, t: (b, t, 0)),
            pl.BlockSpec((1, RET_DK), lambda b, t: (0, 0)),
        ],
        out_specs=(pl.BlockSpec((1, ts, RET_DK), lambda b, t: (b, t, 0)),
                   pl.BlockSpec((1, ts, RET_DK), lambda b, t: (b, t, 0))),
        compiler_params=_cparams(("parallel", "parallel")),
        name="rope_tables",
    )(pos, invf)


def _modulated_norm(x, gain, scale, shift):
    xn = x * lax.rsqrt(jnp.mean(x * x, axis=-1, keepdims=True) + EPS)
    return xn * gain * (1.0 + scale) + shift


def _prenorm_kernel(x_ref, mod_ref, gain_ref, o_ref):
    h = _modulated_norm(x_ref[0], gain_ref[0], mod_ref[0, 0, 1:2, :], mod_ref[0, 0, 0:1, :])
    o_ref[0] = h.astype(BF16)


def _prenorm(x, mod, gain, l):
    bsz, seq, d = x.shape
    tm = min(seq, 2048)
    tok = lambda: pl.BlockSpec((1, tm, d), lambda b, i: (b, i, 0))
    return pl.pallas_call(
        _prenorm_kernel,
        out_shape=jax.ShapeDtypeStruct((bsz, seq, d), BF16),
        grid=(bsz, seq // tm),
        in_specs=[
            tok(),
            pl.BlockSpec((1, 1, 6, d), lambda b, i: (l, b, 0, 0)),
            pl.BlockSpec((1, 1, d), lambda b, i: (l, 0, 0)),
        ],
        out_specs=tok(),
        compiler_params=_cparams(("parallel", "parallel")),
        name="prenorm",
    )(x, mod, gain)


INPROJ_N_SPLIT = 2


def _inproj_kernel(h_ref, w_ref, o_ref, wb_ref):
    @pl.when((pl.program_id(1) == 0) & (pl.program_id(2) == 0))
    def _():
        wb_ref[...] = w_ref[0].astype(BF16)

    part = wb_ref.shape[1] // INPROJ_N_SPLIT
    for n in range(INPROJ_N_SPLIT):
        cs = slice(n * part, (n + 1) * part)
        o_ref[0, :, cs] = _dot(h_ref[0], wb_ref[:, cs]).astype(BF16)


def _inproj(h, w_in, l):
    bsz, seq, d = h.shape
    n = w_in.shape[-1]
    tm = min(seq, 2048)
    tn = 2048
    return pl.pallas_call(
        _inproj_kernel,
        out_shape=jax.ShapeDtypeStruct((bsz, seq, n), BF16),
        grid=(n // tn, bsz, seq // tm),
        in_specs=[
            pl.BlockSpec((1, tm, d), lambda j, b, i: (b, i, 0)),
            pl.BlockSpec((1, d, tn), lambda j, b, i: (l, 0, j)),
        ],
        out_specs=pl.BlockSpec((1, tm, tn), lambda j, b, i: (b, i, j)),
        scratch_shapes=[pltpu.VMEM((d, tn), BF16)],
        compiler_params=_cparams(("arbitrary", "arbitrary", "arbitrary")),
        name="in_proj",
    )(h, w_in)


def _ret_kernel(q_ref, k_ref, v_ref, g_ref, cc_ref, ss_ref, o_ref, state_ref):
    head = pl.program_id(1)

    @pl.when(pl.program_id(2) == 0)
    def _():
        state_ref[...] = jnp.zeros_like(state_ref)

    hv = jnp.zeros((1, 1), F32) + head.astype(F32)
    log_gamma = jnp.log1p(-jnp.exp2(-5.0 - hv))
    ii = lax.broadcasted_iota(jnp.int32, (CHUNK, CHUNK), 0)
    jj = lax.broadcasted_iota(jnp.int32, (CHUNK, CHUNK), 1)
    causal = ii >= jj
    diff = jnp.where(causal, (ii - jj).astype(F32), 0.0)
    decay_intra = jnp.where(causal, jnp.exp(log_gamma * diff), 0.0)
    pos = lax.broadcasted_iota(jnp.int32, (CHUNK, 1), 0).astype(F32)
    decay_q = jnp.exp(log_gamma * (pos + 1.0))
    decay_k = jnp.exp(log_gamma * (CHUNK - 1.0 - pos))
    decay_chunk = jnp.exp(log_gamma * CHUNK)

    cc = cc_ref[0]
    ss = ss_ref[0]
    q = q_ref[0].astype(F32)
    k = k_ref[0].astype(F32)
    q = q * cc + pltpu.roll(q, RET_DK // 2, 1) * ss
    k = (k * cc + pltpu.roll(k, RET_DK // 2, 1) * ss) * RET_DK ** -0.5

    ts = q.shape[0]
    nc = ts // CHUNK
    qb = q.astype(BF16)
    kb = k.astype(BF16)
    qd = (q.reshape(nc, CHUNK, RET_DK) * decay_q).astype(BF16)
    kd = (k.reshape(nc, CHUNK, RET_DK) * decay_k).astype(BF16)
    v = v_ref[0]
    chunks = [slice(c * CHUNK, (c + 1) * CHUNK) for c in range(nc)]
    scores = [_dot_nt(qb[sl], kb[sl]) for sl in chunks]
    kv = [_dot_tn(kd[c], v[sl]) for c, sl in enumerate(chunks)]
    scores = [(s * decay_intra).astype(BF16) for s in scores]
    intra = [_dot(scores[c], v[sl]) for c, sl in enumerate(chunks)]
    state = state_ref[...]
    states = []
    for c in range(nc):
        states.append(state.astype(BF16))
        state = decay_chunk * state + kv[c]
    state_ref[...] = state
    inter = [_dot(qd[c], states[c]) for c in range(nc)]
    out = jnp.concatenate(intra, axis=0) + jnp.concatenate(inter, axis=0)
    out = out * lax.rsqrt(jnp.mean(out * out, axis=-1, keepdims=True) + EPS)
    o_ref[0] = (out * _silu(g_ref[0].astype(F32))).astype(BF16)


def _retention(proj, cc, ss):
    bsz, seq, _ = proj.shape
    ts = min(seq, 2048)
    qb, kb = OFF_RQ // RET_DK, OFF_RK // RET_DK
    vb, gb = OFF_RV // RET_DV, OFF_RG // RET_DV
    return pl.pallas_call(
        _ret_kernel,
        out_shape=jax.ShapeDtypeStruct((bsz, seq, RET_HEADS * RET_DV), BF16),
        grid=(bsz, RET_HEADS, seq // ts),
        in_specs=[
            pl.BlockSpec((1, ts, RET_DK), lambda b, h, t: (b, t, qb + h)),
            pl.BlockSpec((1, ts, RET_DK), lambda b, h, t: (b, t, kb + h)),
            pl.BlockSpec((1, ts, RET_DV), lambda b, h, t: (b, t, vb + h)),
            pl.BlockSpec((1, ts, RET_DV), lambda b, h, t: (b, t, gb + h)),
            pl.BlockSpec((1, ts, RET_DK), lambda b, h, t: (b, t, 0)),
            pl.BlockSpec((1, ts, RET_DK), lambda b, h, t: (b, t, 0)),
        ],
        out_specs=pl.BlockSpec((1, ts, RET_DV), lambda b, h, t: (b, t, h)),
        scratch_shapes=[pltpu.VMEM((RET_DK, RET_DV), F32)],
        compiler_params=_cparams(("parallel", "parallel", "arbitrary")),
        name="retention",
    )(proj, proj, proj, proj, cc, ss)


def _causal_conv(xs_ref, rows, w, b):
    y = b
    for k in range(CONV_W):
        s = CONV_W - 1 - k
        y = y + w[k:k + 1, :] * xs_ref[pl.ds(V7X_SUBLANES - s, rows), :]
    return y


LRU_COLS = 512
LRU_SEGS = V7X_SUBLANES


def _lru_pitch(seg_len):
    assert seg_len % V7X_SUBLANES == 0
    return seg_len + 4


def _lru_kernel(lx_ref, ly_ref, cw_ref, cb_ref, wr_ref, br_ref, wi_ref, bi_ref, lam_ref, o_ref,
                xs_ref, a_ref, u_ref, h_ref, p_ref):
    seq = lx_ref.shape[1]
    nslab = lx_ref.shape[2] // V7X_LANES
    seg_len = seq // LRU_SEGS
    pitch = _lru_pitch(seg_len)
    zeros8 = jnp.zeros((V7X_SUBLANES, V7X_LANES), F32)

    for s in range(nslab):
        cs = slice(s * V7X_LANES, (s + 1) * V7X_LANES)
        xs_ref[s, pl.ds(0, V7X_SUBLANES), :] = zeros8
        xs_ref[s, pl.ds(V7X_SUBLANES, seq), :] = lx_ref[0, :, cs].astype(F32)
        xl = _causal_conv(xs_ref.at[s], seq, cw_ref[0, :, cs], cb_ref[0, :, cs])
        xb = xl.astype(BF16)
        tr = jnp.tanh(_dot(xb, wr_ref[0, s]) + br_ref[0, :, cs])
        ti = jnp.tanh(_dot(xb, wi_ref[0, s]) + bi_ref[0, :, cs])
        lam = lam_ref[0, :, cs]
        softplus_neg_lam = jnp.maximum(-lam, 0.0) + jnp.log1p(jnp.exp(-jnp.abs(lam)))
        log_a = (tr + 1.0) * (-0.5 * LRU_C * softplus_neg_lam)
        a = jnp.exp(log_a)
        y = -jnp.tanh(log_a) * (a * a + 1.0)
        u = jnp.where(y > 0.0, y * lax.rsqrt(y), 0.0) * ((0.5 * ti + 0.5) * xl)
        for g in range(LRU_SEGS):
            a_ref[s, pl.ds(g * pitch, seg_len), :] = a[g * seg_len:(g + 1) * seg_len]
            u_ref[s, pl.ds(g * pitch, seg_len), :] = u[g * seg_len:(g + 1) * seg_len]

    def body(t, carry):
        hs, ps = carry
        new_h, new_p = [], []
        for s in range(nslab):
            rows = pl.ds(t, LRU_SEGS, stride=pitch)
            av = a_ref[s, rows, :]
            h = av * hs[s] + u_ref[s, rows, :]
            p = av * ps[s]
            h_ref[s, rows, :] = h
            p_ref[s, rows, :] = p
            new_h.append(h)
            new_p.append(p)
        return tuple(new_h), tuple(new_p)

    init = (tuple(zeros8 for _ in range(nslab)), tuple(zeros8 + 1.0 for _ in range(nslab)))
    h_end, p_end = lax.fori_loop(0, seg_len, body, init, unroll=8)

    for s in range(nslab):
        cs = slice(s * V7X_LANES, (s + 1) * V7X_LANES)
        carry = jnp.zeros((1, V7X_LANES), F32)
        for g in range(LRU_SEGS):
            rows = pl.ds(g * pitch, seg_len)
            h = h_ref[s, rows, :] + p_ref[s, rows, :] * carry
            gate = jax.nn.gelu(ly_ref[0, g * seg_len:(g + 1) * seg_len, cs].astype(F32))
            o_ref[0, g * seg_len:(g + 1) * seg_len, cs] = (h * gate).astype(BF16)
            carry = h_end[s][g:g + 1] + p_end[s][g:g + 1] * carry


def _lru(proj, conv_w, conv_b, w_r, b_r, w_i, b_i, lam, l):
    bsz, seq, _ = proj.shape
    w = LRU_COLS
    nslab = w // V7X_LANES
    scan_rows = LRU_SEGS * _lru_pitch(seq // LRU_SEGS)
    vec = lambda: pl.BlockSpec((1, 1, w), lambda b, j: (l, 0, j))
    wsp = lambda: pl.BlockSpec((1, nslab, V7X_LANES, V7X_LANES), lambda b, j: (l, j, 0, 0))
    return pl.pallas_call(
        _lru_kernel,
        out_shape=jax.ShapeDtypeStruct((bsz, seq, D_MODEL), BF16),
        grid=(bsz, D_MODEL // w),
        in_specs=[
            pl.BlockSpec((1, seq, w), lambda b, j: (b, 0, OFF_LX // w + j)),
            pl.BlockSpec((1, seq, w), lambda b, j: (b, 0, OFF_LY // w + j)),
            pl.BlockSpec((1, CONV_W, w), lambda b, j: (l, 0, j)),
            vec(), wsp(), vec(), wsp(), vec(), vec(),
        ],
        out_specs=pl.BlockSpec((1, seq, w), lambda b, j: (b, 0, j)),
        scratch_shapes=[pltpu.VMEM((nslab, seq + V7X_SUBLANES, V7X_LANES), F32)]
                       + [pltpu.VMEM((nslab, scan_rows, V7X_LANES), F32)] * 4,
        compiler_params=_cparams(("parallel", "parallel")),
        name="rg_lru",
    )(proj, proj, conv_w, conv_b, w_r, b_r, w_i, b_i, lam)


def _mprep_kernel(mx_ref, cw_ref, cb_ref, wq_ref, wk_ref, wv_ref, wif_ref, bif_ref, *rest):
    nw = (len(rest) - 5) // 2
    wf_refs, (q_ref, k_ref, v_ref, gate_ref) = rest[:nw], rest[nw:nw + 4]
    wb_refs, xs_ref = rest[nw + 4:2 * nw + 4], rest[2 * nw + 4]
    for wf_ref, wb_ref in zip(wf_refs, wb_refs):
        wb_ref[...] = wf_ref[0].astype(BF16)
    ts = mx_ref.shape[1]

    @pl.when(pl.program_id(1) == 0)
    def _():
        for s in range(xs_ref.shape[0]):
            xs_ref[s, pl.ds(ts, V7X_SUBLANES), :] = jnp.zeros((V7X_SUBLANES, V7X_LANES), F32)

    mxb = mx_ref[0]
    xc = []
    for s in range(xs_ref.shape[0]):
        cs = slice(s * V7X_LANES, (s + 1) * V7X_LANES)
        xs_ref[s, pl.ds(0, V7X_SUBLANES), :] = xs_ref[s, pl.ds(ts, V7X_SUBLANES), :]
        xs_ref[s, pl.ds(V7X_SUBLANES, ts), :] = mxb[:, cs].astype(F32)
        y = _causal_conv(xs_ref.at[s], ts, cw_ref[0, :, cs], cb_ref[0, :, cs])
        xc.append(_silu(y).astype(BF16))
    xc = jnp.concatenate(xc, axis=1)
    acc = jnp.zeros((ts, V7X_LANES), F32) + bif_ref[0]
    nh = MLSTM_HEADS
    for h in range(nh):
        cs = slice(h * MLSTM_DH, (h + 1) * MLSTM_DH)
        mq = _dot(xc[:, cs], wq_ref[0, h]).astype(BF16)
        mk = _dot(xc[:, cs], wk_ref[0, h]).astype(BF16)
        mv = _dot(mxb[:, cs], wv_ref[0, h]).astype(BF16)
        q_ref[0, :, cs] = mq
        k_ref[0, :, cs] = mk
        v_ref[0, :, cs] = mv
        acc = acc + _dot(mq, wif_ref[0, h]) + _dot(mk, wif_ref[0, nh + h]) + _dot(mv, wif_ref[0, 2 * nh + h])
    lane = lax.broadcasted_iota(jnp.int32, acc.shape, 1)
    log_f = jnp.minimum(acc, 0.0) - jnp.log1p(jnp.exp(-jnp.abs(acc)))
    gate_ref[0] = jnp.where(lane >= nh, log_f, acc)


def _mlstm_prep(proj, conv_w, conv_b, wq, wk, wv, wif, bif, merge_ws, l):
    bsz, seq, _ = proj.shape
    width = MLSTM_HEADS * MLSTM_DH
    ts = min(seq, 1024)
    nt = seq // ts
    nh, dh = MLSTM_HEADS, MLSTM_DH
    steps = bsz * nt
    qkv = jax.ShapeDtypeStruct((bsz, seq, width), BF16)
    wb_shapes = tuple(jax.ShapeDtypeStruct(w.shape[1:], BF16) for w in merge_ws)
    wf_specs = [pl.BlockSpec((1, w.shape[1] // steps, w.shape[2]), lambda b, t: (l, b * nt + t, 0)) for w in merge_ws]
    wb_specs = tuple(pl.BlockSpec((w.shape[1] // steps, w.shape[2]), lambda b, t: (b * nt + t, 0)) for w in merge_ws)
    wspec = lambda: pl.BlockSpec((1, nh, dh, dh), lambda b, t: (l, 0, 0, 0))
    ospec = lambda: pl.BlockSpec((1, ts, width), lambda b, t: (b, t, 0))
    return pl.pallas_call(
        _mprep_kernel,
        out_shape=(qkv, qkv, qkv, jax.ShapeDtypeStruct((bsz, seq, V7X_LANES), F32)) + wb_shapes,
        grid=(bsz, nt),
        in_specs=[
            pl.BlockSpec((1, ts, width), lambda b, t: (b, t, OFF_MX // width)),
            pl.BlockSpec((1, CONV_W, width), lambda b, t: (l, 0, 0)),
            pl.BlockSpec((1, 1, width), lambda b, t: (l, 0, 0)),
            wspec(), wspec(), wspec(),
            pl.BlockSpec((1, 3 * nh, dh, V7X_LANES), lambda b, t: (l, 0, 0, 0)),
            pl.BlockSpec((1, 1, V7X_LANES), lambda b, t: (l, 0, 0)),
        ] + wf_specs,
        out_specs=(ospec(), ospec(), ospec(),
                   pl.BlockSpec((1, ts, V7X_LANES), lambda b, t: (b, t, 0))) + wb_specs,
        scratch_shapes=[pltpu.VMEM((width // V7X_LANES, ts + V7X_SUBLANES, V7X_LANES), F32)],
        compiler_params=_cparams(("parallel", "arbitrary")),
        name="mlstm_prep",
    )(proj, conv_w, conv_b, wq, wk, wv, wif, bif, *merge_ws)


def _mlstm_kernel(q_ref, k_ref, v_ref, gate_ref, mo_ref, mn_ref, o_ref, c_ref, n_ref, m_ref):
    head = pl.program_id(1)

    @pl.when(pl.program_id(2) == 0)
    def _():
        c_ref[...] = jnp.zeros_like(c_ref)
        n_ref[...] = jnp.zeros_like(n_ref)
        m_ref[...] = jnp.zeros_like(m_ref)

    gates = gate_ref[0]
    ts = gates.shape[0]
    nc = ts // CHUNK
    lane = lax.broadcasted_iota(jnp.int32, gates.shape, 1)
    ic = jnp.sum(jnp.where(lane == head, gates, 0.0), axis=1, keepdims=True).reshape(nc, CHUNK, 1)
    lf = jnp.sum(jnp.where(lane == head + MLSTM_HEADS, gates, 0.0), axis=1, keepdims=True).reshape(nc, CHUNK, 1)

    ii = lax.broadcasted_iota(jnp.int32, (nc, CHUNK, CHUNK), 1)
    jj = lax.broadcasted_iota(jnp.int32, (nc, CHUNK, CHUNK), 2)
    causal = ii >= jj
    diag = ii == jj
    b_row = jnp.sum(jnp.where(ii <= jj, lf, 0.0), axis=1, keepdims=True)
    lf_row = jnp.sum(jnp.where(diag, lf, 0.0), axis=1, keepdims=True)
    ic_row = jnp.sum(jnp.where(diag, ic, 0.0), axis=1, keepdims=True)
    b_col = jnp.sum(jnp.where(causal, lf_row, 0.0), axis=2, keepdims=True)
    dmat = jnp.where(causal, b_col - b_row + ic_row, -jnp.inf)
    row_max = jnp.max(dmat, axis=2, keepdims=True)

    m_s = m_ref[...]
    m_t_list, m_prev_list = [], []
    for c in range(nc):
        m_prev_list.append(m_s)
        m_tc = jnp.maximum(b_col[c] + m_s, row_max[c])
        m_t_list.append(m_tc)
        m_s = m_tc[CHUNK - 1:]
    m_ref[...] = m_s

    q = q_ref[0]
    v = v_ref[0]
    kb = k_ref[0] * jnp.asarray(MLSTM_DH ** -0.5, BF16)
    m_t = jnp.stack(m_t_list)
    m_prev = jnp.stack(m_prev_list)
    w_inter = jnp.exp(b_col + m_prev - m_t)
    b_last = b_col[:, CHUNK - 1:, :]
    m_new = m_t[:, CHUNK - 1:, :]
    w_k = jnp.exp(b_last - b_col + ic - m_new)
    decay = jnp.exp(b_last + m_prev - m_new)
    p = jnp.exp(dmat - m_t)
    kwb = kb.reshape(nc, CHUNK, MLSTM_DH) * w_k.astype(BF16)
    ones = jnp.ones((V7X_SUBLANES, CHUNK), BF16)

    chunks = [slice(c * CHUNK, (c + 1) * CHUNK) for c in range(nc)]
    qk = [_dot_nt(q[sl], kb[sl]) for sl in chunks]
    kv = [_dot_tn(kwb[c], v[sl]) for c, sl in enumerate(chunks)]
    n_add = [_dot(ones, kwb[c]) for c in range(nc)]
    s = [qk[c] * p[c] for c in range(nc)]
    intra = [_dot(s[c].astype(BF16), v[sl]) for c, sl in enumerate(chunks)]
    c_s = c_ref[...]
    n_s = n_ref[...]
    c_states, n_states = [], []
    for c in range(nc):
        c_states.append(c_s.astype(BF16))
        n_states.append(n_s)
        c_s = decay[c] * c_s + kv[c]
        n_s = decay[c] * n_s + n_add[c][:1]
    c_ref[...] = c_s
    n_ref[...] = n_s
    inter = [_dot(q[sl], c_states[c]) for c, sl in enumerate(chunks)]
    w_inter = w_inter.reshape(ts, 1)
    num = jnp.concatenate(intra, axis=0) + w_inter * jnp.concatenate(inter, axis=0)
    s_sum = jnp.sum(jnp.stack(s), axis=2, keepdims=True).reshape(ts, 1)
    qn = [_dot_nt(q[sl], jnp.broadcast_to(n_states[c], (V7X_SUBLANES, MLSTM_DH)).astype(BF16))[:, :1]
          for c, sl in enumerate(chunks)]
    den = s_sum + w_inter * jnp.concatenate(qn, axis=0)
    h = num / jnp.maximum(jnp.abs(den), jnp.exp(-m_t.reshape(ts, 1)))
    o = _sigmoid(mo_ref[0].astype(F32)) * h
    o = o * lax.rsqrt(jnp.mean(o * o, axis=-1, keepdims=True) + EPS)
    o_ref[0] = (o * mn_ref[0]).astype(BF16)


def _mlstm(mq, mk, mv, gates, proj, m_norm, l):
    bsz, seq, width = mq.shape
    dh = MLSTM_DH
    ts = min(seq, 2048)
    hspec = lambda: pl.BlockSpec((1, ts, dh), lambda b, h, t: (b, t, h))
    return pl.pallas_call(
        _mlstm_kernel,
        out_shape=jax.ShapeDtypeStruct((bsz, seq, width), BF16),
        grid=(bsz, MLSTM_HEADS, seq // ts),
        in_specs=[
            hspec(), hspec(), hspec(),
            pl.BlockSpec((1, ts, V7X_LANES), lambda b, h, t: (b, t, 0)),
            pl.BlockSpec((1, ts, dh), lambda b, h, t: (b, t, OFF_MO // dh + h)),
            pl.BlockSpec((1, 1, dh), lambda b, h, t: (l, 0, h)),
        ],
        out_specs=hspec(),
        scratch_shapes=[pltpu.VMEM((dh, dh), F32), pltpu.VMEM((1, dh), F32), pltpu.VMEM((1, 1), F32)],
        compiler_params=_cparams(("parallel", "parallel", "arbitrary")),
        name="mlstm",
    )(mq, mk, mv, gates, proj, m_norm)


FFN_HIDDEN_SPLIT = 4


def _merge_ffn_kernel(x_ref, ret_ref, lru_ref, mls_ref, g0_ref, g1_ref, g2_ref, mod_ref, gain_ref,
                      wr_ref, wl_ref, wm_ref, wo_ref, w1_ref, w2_ref, *rest, last):
    merged = (_sigmoid(g0_ref[0].astype(F32)) * _dot(ret_ref[0], wr_ref[...])
              + _sigmoid(g1_ref[0].astype(F32)) * _dot(lru_ref[0], wl_ref[...])
              + _sigmoid(g2_ref[0].astype(F32)) * _dot(mls_ref[0], wm_ref[...]))
    x = x_ref[0] + mod_ref[0, 0, 2:3, :] * _dot(merged.astype(BF16), wo_ref[...])
    h = _modulated_norm(x, gain_ref[0], mod_ref[0, 0, 4:5, :], mod_ref[0, 0, 3:4, :])
    hb = h.astype(BF16)
    part = w1_ref.shape[1] // FFN_HIDDEN_SPLIT
    ff = None
    for n in range(FFN_HIDDEN_SPLIT):
        cs = slice(n * part, (n + 1) * part)
        a = jnp.square(jnp.maximum(_dot(hb, w1_ref[:, cs]), 0.0))
        t = _dot(a.astype(BF16), w2_ref[cs, :])
        ff = t if ff is None else ff + t
    y = x + mod_ref[0, 0, 5:6, :] * ff
    if last:
        fgain_ref, o_ref = rest
        o_ref[0] = y * lax.rsqrt(jnp.mean(y * y, axis=-1, keepdims=True) + EPS) * fgain_ref[...]
    else:
        nmod_ref, ngain_ref, o_ref, hn_ref = rest
        o_ref[0] = y
        hn = _modulated_norm(y, ngain_ref[0], nmod_ref[0, 0, 1:2, :], nmod_ref[0, 0, 0:1, :])
        hn_ref[0] = hn.astype(BF16)


def _merge_ffn(x, ret, lru, mls, proj, mod, gain, w_br_ret, w_br_lru, w_br_mlstm, w_out, w1, w2,
               mix_gain, final_gain, l, last):
    bsz, seq, d = x.shape
    dff = w1.shape[-1]
    tm = min(seq, 512)
    gb = OFF_GATE // d
    tok = lambda: pl.BlockSpec((1, tm, d), lambda b, i: (b, i, 0))
    once = lambda shape: pl.BlockSpec(shape, lambda b, i: (0, 0), pipeline_mode=pl.Buffered(1))
    in_specs = [
        tok(), tok(), tok(), tok(),
        pl.BlockSpec((1, tm, d), lambda b, i: (b, i, gb)),
        pl.BlockSpec((1, tm, d), lambda b, i: (b, i, gb + 1)),
        pl.BlockSpec((1, tm, d), lambda b, i: (b, i, gb + 2)),
        pl.BlockSpec((1, 1, 6, d), lambda b, i: (l, b, 0, 0)),
        pl.BlockSpec((1, 1, d), lambda b, i: (l, 0, 0)),
        once((d, d)), once((d, d)), once((d, d)), once((d, d)), once((d, dff)), once((dff, d)),
    ]
    if last:
        in_specs.append(pl.BlockSpec((1, d), lambda b, i: (0, 0)))
        extra = (final_gain,)
        out_shape = jax.ShapeDtypeStruct((bsz, seq, d), F32)
        out_specs = tok()
    else:
        in_specs += [pl.BlockSpec((1, 1, 6, d), lambda b, i: (l + 1, b, 0, 0)),
                     pl.BlockSpec((1, 1, d), lambda b, i: (l + 1, 0, 0))]
        extra = (mod, mix_gain)
        out_shape = (jax.ShapeDtypeStruct((bsz, seq, d), F32), jax.ShapeDtypeStruct((bsz, seq, d), BF16))
        out_specs = (tok(), tok())
    return pl.pallas_call(
        functools.partial(_merge_ffn_kernel, last=last),
        out_shape=out_shape,
        grid=(bsz, seq // tm),
        in_specs=in_specs,
        out_specs=out_specs,
        compiler_params=_cparams(("parallel", "parallel")),
        name="merge_ffn",
    )(x, ret, lru, mls, proj, proj, proj, mod, gain, w_br_ret, w_br_lru, w_br_mlstm, w_out, w1, w2, *extra)


def _block_diag_tiles(w, tile):
    depth, nb, bs, _ = w.shape
    rows = nb * bs
    sel = jnp.tile(jnp.eye(bs, dtype=w.dtype), (1, tile // bs))
    dense = jnp.einsum('lre,ec->lrc', w.reshape(depth, rows, bs), sel, precision=lax.Precision.HIGHEST)
    r = (np.arange(rows) % tile) // bs
    c = np.arange(tile) // bs
    dense = jnp.where(jnp.asarray(r[:, None] == c[None, :]), dense, 0.0)
    return dense.reshape(depth, rows // tile, tile, tile)


def kernel(x, c, positions, w_ada, b_ada, norm_mix, norm_mlp, w_in, lru_conv_w, lru_conv_b, lru_w_r, lru_b_r, lru_w_i, lru_b_i, lru_lambda, m_conv_w, m_conv_b, m_w_q, m_w_k, m_w_v, m_w_if, m_b_if, m_norm, w_br_ret, w_br_lru, w_br_mlstm, w_out, w_ff1, w_ff2, final_norm):
    depth = w_in.shape[0]
    bsz, seq, d = x.shape
    nh, dh = MLSTM_HEADS, MLSTM_DH

    mod = _ada(c, w_ada, b_ada).reshape(depth, bsz, 6, d)
    cc, ss = _rope_tables(positions)

    vec = lambda a: a.reshape(depth, 1, a.shape[-1])
    w_r_t = (0.5 * _block_diag_tiles(lru_w_r, V7X_LANES)).astype(BF16)
    w_i_t = (0.5 * _block_diag_tiles(lru_w_i, V7X_LANES)).astype(BF16)
    wq_t = _block_diag_tiles(m_w_q, dh).astype(BF16)
    wk_t = _block_diag_tiles(m_w_k, dh).astype(BF16)
    wv_t = _block_diag_tiles(m_w_v, dh).astype(BF16)
    wif_t = jnp.pad(m_w_if, ((0, 0), (0, 0), (0, V7X_LANES - 2 * nh))).reshape(depth, 3 * nh, dh, V7X_LANES).astype(BF16)
    bif_t = jnp.pad(m_b_if, ((0, 0), (0, V7X_LANES - 2 * nh))).reshape(depth, 1, V7X_LANES)
    g_mix, g_mlp = vec(norm_mix), vec(norm_mlp)
    l_cb, l_br, l_bi, l_lam = vec(lru_conv_b), vec(0.5 * lru_b_r), vec(0.5 * lru_b_i), vec(lru_lambda)
    m_cb, m_nrm = vec(m_conv_b), vec(m_norm)
    f_gain = final_norm.reshape(1, d)

    h = _prenorm(x, mod, g_mix, 0)
    for l in range(depth):
        proj = _inproj(h, w_in, l)
        ret = _retention(proj, cc, ss)
        lru = _lru(proj, lru_conv_w, l_cb, w_r_t, l_br, w_i_t, l_bi, l_lam, l)
        mq, mk, mv, gates, *wb = _mlstm_prep(
            proj, m_conv_w, m_cb, wq_t, wk_t, wv_t, wif_t, bif_t,
            (w_br_ret, w_br_lru, w_br_mlstm, w_out, w_ff1, w_ff2), l)
        mls = _mlstm(mq, mk, mv, gates, proj, m_nrm, l)
        if l == depth - 1:
            return _merge_ffn(x, ret, lru, mls, proj, mod, g_mlp, *wb, g_mix, f_gain, l, last=True)
        x, h = _merge_ffn(x, ret, lru, mls, proj, mod, g_mlp, *wb, g_mix, f_gain, l, last=False)
```
